```python
import math
import jax
import jax.numpy as jnp
from jax import lax
import numpy as np

D_MODEL = 2048
BATCH = 8
SEQ = 2048
DEPTH = 2

MIX_WIDTH = D_MODEL // 2
N_BRANCH = 3
HEAD_DIM = 128
FOX_HEADS = MIX_WIDTH // HEAD_DIM
RET_HEADS = MIX_WIDTH // HEAD_DIM
SSM_GROUP = 16
SSM_GROUPS = MIX_WIDTH // SSM_GROUP
SSM_STATE = 64
D_FF = ((8 * D_MODEL // 3 + 127) // 128) * 128
N_MOD = 9
Q_BLOCK = 128
RET_CHUNK = 128
RET_DECAY_BASE = 5.0
ROPE_BASE = 10000.0
EPS = 1e-6
DT_MIN = 0.001
DT_MAX = 0.1
IN_SPLITS = (MIX_WIDTH, MIX_WIDTH, MIX_WIDTH, FOX_HEADS, MIX_WIDTH,
             MIX_WIDTH, MIX_WIDTH, MIX_WIDTH, MIX_WIDTH, N_BRANCH * D_MODEL)
IN_WIDTH = 8 * MIX_WIDTH + FOX_HEADS + N_BRANCH * D_MODEL

kernel_name = "fox_s5_retnet_macaron_adaln_hybrid"


def rmsnorm(x, w):
    xf = x.astype(jnp.float32)
    y = xf * lax.rsqrt(jnp.mean(xf * xf, axis=-1, keepdims=True) + EPS)
    return (y * w.astype(jnp.float32)).astype(x.dtype)


def modulate(h, shift, scale):
    return h * (1.0 + scale[:, None, :]) + shift[:, None, :]


def swiglu(h, w1, w3, w2):
    return (jax.nn.silu(h @ w1) * (h @ w3)) @ w2


def split_cols(z, sizes):
    idx = []
    acc = 0
    for s in sizes[:-1]:
        acc += s
        idx.append(acc)
    return jnp.split(z, idx, axis=-1)


def fox_attention(q, k, v, f_logit):
    bsz, s_len, _ = q.shape
    nb = s_len // Q_BLOCK

    def heads(t):
        return t.reshape(bsz, s_len, FOX_HEADS, HEAD_DIM).transpose(0, 2, 1, 3)

    qh = heads(q) * (HEAD_DIM ** -0.5)
    kh = heads(k)
    vh = heads(v)
    cum_logf = jnp.cumsum(jax.nn.log_sigmoid(f_logit.astype(jnp.float32)), axis=1).transpose(0, 2, 1)
    q_blocks = qh.reshape(bsz, FOX_HEADS, nb, Q_BLOCK, HEAD_DIM).transpose(2, 0, 1, 3, 4)
    f_blocks = cum_logf.reshape(bsz, FOX_HEADS, nb, Q_BLOCK).transpose(2, 0, 1, 3)
    key_pos = jnp.arange(s_len)

    def one_block(args):
        q_i, f_i, i = args
        logits = jnp.einsum('bhqd,bhkd->bhqk', q_i, kh).astype(jnp.float32)
        logits = logits + f_i[..., :, None] - cum_logf[..., None, :]
        q_pos = i * Q_BLOCK + jnp.arange(Q_BLOCK)
        causal = key_pos[None, :] <= q_pos[:, None]
        logits = jnp.where(causal, logits, -jnp.inf)
        probs = jax.nn.softmax(logits, axis=-1).astype(vh.dtype)
        return jnp.einsum('bhqk,bhkd->bhqd', probs, vh)

    out = lax.map(one_block, (q_blocks, f_blocks, jnp.arange(nb)))
    return out.transpose(1, 0, 3, 2, 4).reshape(bsz, s_len, MIX_WIDTH)


def s5_ssm(u, a_re, a_im, log_dt, b_re, b_im, c_re, c_im, d_skip):
    f32 = jnp.float32
    bsz, s_len, _ = u.shape
    a_re = a_re.astype(f32)
    a_im = a_im.astype(f32)
    u_g = u.astype(f32).reshape(bsz, s_len, SSM_GROUPS, SSM_GROUP).transpose(1, 0, 2, 3)
    dt = jnp.exp(log_dt.astype(f32))[:, None]
    mag = jnp.exp(a_re * dt)
    ab_re = mag * jnp.cos(a_im * dt)
    ab_im = mag * jnp.sin(a_im * dt)
    den = a_re * a_re + a_im * a_im
    num_re = ab_re - 1.0
    coef_re = (num_re * a_re + ab_im * a_im) / den
    coef_im = (ab_im * a_re - num_re * a_im) / den
    b_re = b_re.astype(f32)
    b_im = b_im.astype(f32)
    bb_re = coef_re[..., None] * b_re - coef_im[..., None] * b_im
    bb_im = coef_re[..., None] * b_im + coef_im[..., None] * b_re
    bu_re = jnp.einsum('sbgn,gpn->sbgp', u_g, bb_re)
    bu_im = jnp.einsum('sbgn,gpn->sbgp', u_g, bb_im)
    a_seq_re = jnp.broadcast_to(ab_re, (s_len,) + ab_re.shape)
    a_seq_im = jnp.broadcast_to(ab_im, (s_len,) + ab_im.shape)

    def combine(e1, e2):
        a1r, a1i, b1r, b1i = e1
        a2r, a2i, b2r, b2i = e2
        ar = a1r * a2r - a1i * a2i
        ai = a1r * a2i + a1i * a2r
        br = a2r[:, None] * b1r - a2i[:, None] * b1i + b2r
        bi = a2r[:, None] * b1i + a2i[:, None] * b1r + b2i
        return ar, ai, br, bi

    _, _, x_re, x_im = lax.associative_scan(combine, (a_seq_re, a_seq_im, bu_re, bu_im), axis=0)
    y = (jnp.einsum('sbgp,gnp->sbgn', x_re, c_re.astype(f32))
         - jnp.einsum('sbgp,gnp->sbgn', x_im, c_im.astype(f32)))
    y = y + d_skip.astype(f32).reshape(SSM_GROUPS, SSM_GROUP) * u_g
    return y.transpose(1, 0, 2, 3).reshape(bsz, s_len, MIX_WIDTH)


def retention(q, k, v):
    f32 = jnp.float32
    bsz, s_len, _ = q.shape
    nc = s_len // RET_CHUNK
    half = HEAD_DIM // 2
    pos = jnp.arange(s_len, dtype=f32)
    inv_freq = 1.0 / (ROPE_BASE ** jnp.linspace(0.0, 1.0, half, dtype=f32))
    ang = pos[:, None] * inv_freq[None, :]
    cos = jnp.cos(ang)[:, None, :]
    sin = jnp.sin(ang)[:, None, :]

    def heads(t):
        return t.astype(f32).reshape(bsz, s_len, RET_HEADS, HEAD_DIM)

    def rot(t):
        t1, t2 = t[..., :half], t[..., half:]
        return jnp.concatenate([t1 * cos - t2 * sin, t1 * sin + t2 * cos], axis=-1)

    def chunks(t):
        return t.reshape(bsz, nc, RET_CHUNK, RET_HEADS, HEAD_DIM).transpose(1, 0, 3, 2, 4)

    qh = rot(heads(q))
    kh = rot(heads(k)) * (HEAD_DIM ** -0.5)
    vh = heads(v)
    log_gamma = jnp.log(1.0 - jnp.exp2(-RET_DECAY_BASE - jnp.arange(RET_HEADS, dtype=f32)))
    idx = jnp.arange(RET_CHUNK, dtype=f32)
    rel = idx[:, None] - idx[None, :]
    intra = jnp.where(rel[None] >= 0, jnp.exp(jnp.maximum(rel, 0.0)[None] * log_gamma[:, None, None]), 0.0)
    q_decay = jnp.exp((idx + 1.0)[None, :] * log_gamma[:, None])
    k_decay = jnp.exp((RET_CHUNK - 1.0 - idx)[None, :] * log_gamma[:, None])
    chunk_decay = jnp.exp(RET_CHUNK * log_gamma)

    def step(state, qkv):
        q_c, k_c, v_c = qkv
        scores = jnp.einsum('bhnd,bhmd->bhnm', q_c, k_c) * intra
        o = (jnp.einsum('bhnm,bhme->bhne', scores, v_c)
             + jnp.einsum('bhnd,bhde->bhne', q_c * q_decay[..., None], state))
        state = chunk_decay[:, None, None] * state + jnp.einsum('bhmd,bhme->bhde', k_c * k_decay[..., None], v_c)
        return state, o

    state0 = jnp.zeros((bsz, RET_HEADS, HEAD_DIM, HEAD_DIM), f32)
    _, o = lax.scan(step, state0, (chunks(qh), chunks(kh), chunks(vh)))
    o = o.transpose(1, 0, 3, 2, 4)
    mu = jnp.mean(o, axis=-1, keepdims=True)
    var = jnp.mean(jnp.square(o - mu), axis=-1, keepdims=True)
    o = (o - mu) * lax.rsqrt(var + EPS)
    return o.reshape(bsz, s_len, MIX_WIDTH)


def token_mixing(h, w_in, f_bias, a_re, a_im, log_dt, b_re, b_im, c_re, c_im, d_skip,
                 glu_w, glu_b, gn_w, w_branch, b_gate, w_out):
    bsz, s_len, _ = h.shape
    z = h @ w_in
    qa, ka, va, fa, us, qr, kr, vr, gr, gl = split_cols(z, IN_SPLITS)
    y_fox = fox_attention(qa, ka, va, fa + f_bias)
    y_ssm = jax.nn.gelu(s5_ssm(us, a_re, a_im, log_dt, b_re, b_im, c_re, c_im, d_skip).astype(h.dtype))
    y_ssm = y_ssm * jax.nn.sigmoid(y_ssm @ glu_w + glu_b)
    y_ret = (retention(qr, kr, vr) * gn_w).astype(h.dtype) * jax.nn.silu(gr)
    gates = jax.nn.sigmoid(gl.reshape(bsz, s_len, N_BRANCH, D_MODEL) + b_gate)
    merged = (gates[:, :, 0] * (y_fox @ w_branch[0])
              + gates[:, :, 1] * (y_ssm @ w_branch[1])
              + gates[:, :, 2] * (y_ret @ w_branch[2]))
    return merged @ w_out


def setup_inputs(seed: int = 0) -> dict:
    key = jax.random.key(seed)
    ks = jax.random.split(key, 26)
    f32 = jnp.float32
    L, D, W, F = DEPTH, D_MODEL, MIX_WIDTH, D_FF
    G, P, N = SSM_GROUPS, SSM_STATE, SSM_GROUP

    def nrm(k, shape, scale):
        return jax.random.normal(k, shape, f32) * scale

    n_idx = jnp.arange(P, dtype=f32)
    return {
        "x": nrm(ks[0], (BATCH, SEQ, D), 1.0),
        "c": nrm(ks[1], (BATCH, D), 1.0),
        "ada_w": nrm(ks[2], (L, D, N_MOD * D), 0.5 * D ** -0.5),
        "ada_b": nrm(ks[3], (L, N_MOD * D), 0.02),
        "norm_w": 1.0 + nrm(ks[4], (L, 3, D), 0.02),
        "final_norm_w": 1.0 + nrm(ks[5], (D,), 0.02),
        "ffn_w1": nrm(ks[6], (L, 2, D, F), D ** -0.5),
        "ffn_w3": nrm(ks[7], (L, 2, D, F), D ** -0.5),
        "ffn_w2": nrm(ks[8], (L, 2, F, D), F ** -0.5),
        "w_in": nrm(ks[9], (L, D, IN_WIDTH), D ** -0.5),
        "fox_f_bias": 3.0 + nrm(ks[10], (L, FOX_HEADS), 0.5),
        "ssm_A_re": -0.5 * jnp.exp(nrm(ks[11], (L, G, P), 0.05)),
        "ssm_A_im": math.pi * n_idx + nrm(ks[12], (L, G, P), 0.05),
        "ssm_log_dt": jax.random.uniform(ks[13], (L, G), f32, math.log(DT_MIN), math.log(DT_MAX)),
        "ssm_B_re": nrm(ks[14], (L, G, P, N), (2 * N) ** -0.5),
        "ssm_B_im": nrm(ks[15], (L, G, P, N), (2 * N) ** -0.5),
        "ssm_C_re": nrm(ks[16], (L, G, N, P), P ** -0.5),
        "ssm_C_im": nrm(ks[17], (L, G, N, P), P ** -0.5),
        "ssm_D": nrm(ks[18], (L, W), 1.0),
        "glu_w": nrm(ks[19], (L, W, W), W ** -0.5),
        "glu_b": nrm(ks[20], (L, W), 0.02),
        "ret_gn_w": 1.0 + nrm(ks[21], (L, W), 0.02),
        "w_branch": nrm(ks[22], (L, N_BRANCH, W, D), W ** -0.5),
        "b_gate": nrm(ks[23], (L, N_BRANCH, D), 0.02),
        "w_out": nrm(ks[24], (L, D, D), D ** -0.5),
    }


def reference(x, c, ada_w, ada_b, norm_w, final_norm_w, ffn_w1, ffn_w3, ffn_w2, w_in, fox_f_bias,
              ssm_A_re, ssm_A_im, ssm_log_dt, ssm_B_re, ssm_B_im, ssm_C_re, ssm_C_im, ssm_D,
              glu_w, glu_b, ret_gn_w, w_branch, b_gate, w_out):
    c_act = jax.nn.silu(c)
    for l in range(DEPTH):
        mod = c_act @ ada_w[l] + ada_b[l]
        sh1, sc1, g1, sh2, sc2, g2, sh3, sc3, g3 = jnp.split(mod, N_MOD, axis=-1)
        h = modulate(rmsnorm(x, norm_w[l, 0]), sh1, sc1)
        x = x + 0.5 * g1[:, None, :] * swiglu(h, ffn_w1[l, 0], ffn_w3[l, 0], ffn_w2[l, 0])
        h = modulate(rmsnorm(x, norm_w[l, 1]), sh2, sc2)
        y = token_mixing(h, w_in[l], fox_f_bias[l], ssm_A_re[l], ssm_A_im[l], ssm_log_dt[l],
                         ssm_B_re[l], ssm_B_im[l], ssm_C_re[l], ssm_C_im[l], ssm_D[l],
                         glu_w[l], glu_b[l], ret_gn_w[l], w_branch[l], b_gate[l], w_out[l])
        x = x + g2[:, None, :] * y.astype(x.dtype)
        h = modulate(rmsnorm(x, norm_w[l, 2]), sh3, sc3)
        x = x + 0.5 * g3[:, None, :] * swiglu(h, ffn_w1[l, 1], ffn_w3[l, 1], ffn_w2[l, 1])
    return rmsnorm(x, final_norm_w)
```

```python
import functools
import math

import jax
import jax.numpy as jnp
from jax import lax
from jax.experimental import pallas as pl
from jax.experimental.pallas import tpu as pltpu

F32 = jnp.float32
BF16 = jnp.bfloat16

EPS = 1e-6
HEAD_DIM = 128
SSM_GROUP = 16
SSM_STATE = 64
N_MOD = 9
RET_CHUNK = 128
RET_DECAY_BASE = 5.0
ROPE_BASE = 10000.0
LANES = 128
SUBLANES = 8
V7X_VMEM_BUDGET = 56 * 1024 * 1024
NEG_BIG = -1e30


def _cparams(semantics, vmem_bytes):
    vmem = int(min(max(vmem_bytes, 16 * 1024 * 1024), V7X_VMEM_BUDGET))
    return pltpu.CompilerParams(dimension_semantics=semantics, vmem_limit_bytes=vmem)


def _dot(a, b):
    return jnp.dot(a, b, preferred_element_type=F32)


def _norm_mod(x, nw, sh, sc):
    ms = jnp.mean(x * x, axis=-1, keepdims=True)
    return (x * lax.rsqrt(ms + EPS)) * nw * (1.0 + sc) + sh


def _ada_kernel(c_ref, w_ref, b_ref, o_ref):
    c = c_ref[...]
    ca = (c * jax.nn.sigmoid(c)).astype(BF16)
    o_ref[...] = _dot(ca, w_ref[...].astype(BF16)) + b_ref[...]


def _ada(c, w, b):
    bsz, d = c.shape
    n = w.shape[1]
    tn = 1024 if n % 1024 == 0 else n
    return pl.pallas_call(
        _ada_kernel,
        grid=(n // tn,),
        in_specs=[pl.BlockSpec((bsz, d), lambda j: (0, 0)),
                  pl.BlockSpec((d, tn), lambda j: (0, j)),
                  pl.BlockSpec((1, tn), lambda j: (0, j))],
        out_specs=pl.BlockSpec((bsz, tn), lambda j: (0, j)),
        out_shape=jax.ShapeDtypeStruct((bsz, n), F32),
        compiler_params=_cparams(("arbitrary",), 4 * d * tn * 4),
        name="ada_mod",
    )(c, w, b.reshape(1, n))


def _ffn_kernel(x_ref, mod_ref, nw_ref, w1_ref, w3_ref, w2_ref, *rest, mod_base, final):
    if final:
        fw_ref, o_ref, h_ref = rest
    else:
        o_ref, h_ref = rest
    j = pl.program_id(1)

    @pl.when(j == 0)
    def _():
        sh = mod_ref[0, mod_base:mod_base + 1, :]
        sc = mod_ref[0, mod_base + 1:mod_base + 2, :]
        h_ref[...] = _norm_mod(x_ref[...], nw_ref[...], sh, sc).astype(BF16)
        o_ref[...] = jnp.zeros_like(o_ref)

    h = h_ref[...]
    a = _dot(h, w1_ref[...])
    b = _dot(h, w3_ref[...])
    g = (a * jax.nn.sigmoid(a) * b).astype(BF16)
    o_ref[...] += _dot(g, w2_ref[...])

    @pl.when(j == pl.num_programs(1) - 1)
    def _():
        gate = mod_ref[0, mod_base + 2:mod_base + 3, :]
        y = x_ref[...] + 0.5 * gate * o_ref[...]
        if final:
            ms = jnp.mean(y * y, axis=-1, keepdims=True)
            y = y * lax.rsqrt(ms + EPS) * fw_ref[...]
        o_ref[...] = y


def _ffn(x2, mod3, nw, w1, w3, w2, final_w, *, seq, mod_base, tm, tf):
    t, d = x2.shape
    fp = w1.shape[1]
    per_b = seq // tm
    final = final_w is not None
    in_specs = [pl.BlockSpec((tm, d), lambda i, j: (i, 0)),
                pl.BlockSpec((1, N_MOD, d), lambda i, j: (i // per_b, 0, 0)),
                pl.BlockSpec((1, d), lambda i, j: (0, 0)),
                pl.BlockSpec((d, tf), lambda i, j: (0, j)),
                pl.BlockSpec((d, tf), lambda i, j: (0, j)),
                pl.BlockSpec((tf, d), lambda i, j: (j, 0))]
    args = [x2, mod3, nw.reshape(1, d), w1, w3, w2]
    if final:
        in_specs.append(pl.BlockSpec((1, d), lambda i, j: (0, 0)))
        args.append(final_w.reshape(1, d))
    vmem = 4 * tm * d * 4 + tm * d * 2 + 6 * d * tf * 2 + 3 * tm * tf * 4 + 2 * tm * d * 4
    return pl.pallas_call(
        functools.partial(_ffn_kernel, mod_base=mod_base, final=final),
        grid=(t // tm, fp // tf),
        in_specs=in_specs,
        out_specs=pl.BlockSpec((tm, d), lambda i, j: (i, 0)),
        out_shape=jax.ShapeDtypeStruct((t, d), F32),
        scratch_shapes=[pltpu.VMEM((tm, d), BF16)],
        compiler_params=_cparams(("parallel", "arbitrary"), vmem),
        name="ffn_final" if final else "ffn",
    )(*args)


def _proj_kernel(x_ref, mod_ref, nw_ref, w_ref, wf_ref, z_ref, f_ref, h_ref, *, mod_base):
    j = pl.program_id(1)

    @pl.when(j == 0)
    def _():
        sh = mod_ref[0, mod_base:mod_base + 1, :]
        sc = mod_ref[0, mod_base + 1:mod_base + 2, :]
        h = _norm_mod(x_ref[...], nw_ref[...], sh, sc).astype(BF16)
        h_ref[...] = h
        f_ref[...] = _dot(h, wf_ref[...])

    z_ref[...] = _dot(h_ref[...], w_ref[...]).astype(BF16)


def _proj(x2, mod3, nw, w_main, w_f, *, seq, mod_base, tm, tn):
    t, d = x2.shape
    nz = w_main.shape[1]
    per_b = seq // tm
    vmem = 2 * tm * d * 4 + tm * d * 2 + 2 * d * tn * 2 + 2 * tm * tn * 2 + 2 * (tm * LANES * 4 + d * LANES * 2) \
        + 2 * tm * tn * 4 + tm * d * 4
    return pl.pallas_call(
        functools.partial(_proj_kernel, mod_base=mod_base),
        grid=(t // tm, nz // tn),
        in_specs=[pl.BlockSpec((tm, d), lambda i, j: (i, 0)),
                  pl.BlockSpec((1, N_MOD, d), lambda i, j: (i // per_b, 0, 0)),
                  pl.BlockSpec((1, d), lambda i, j: (0, 0)),
                  pl.BlockSpec((d, tn), lambda i, j: (0, j)),
                  pl.BlockSpec((d, LANES), lambda i, j: (0, 0))],
        out_specs=[pl.BlockSpec((tm, tn), lambda i, j: (i, j)),
                   pl.BlockSpec((tm, LANES), lambda i, j: (i, 0))],
        out_shape=[jax.ShapeDtypeStruct((t, nz), BF16),
                   jax.ShapeDtypeStruct((t, LANES), F32)],
        scratch_shapes=[pltpu.VMEM((tm, d), BF16)],
        compiler_params=_cparams(("parallel", "arbitrary"), vmem),
        name="in_proj",
    )(x2, mod3, nw.reshape(1, d), w_main, w_f)


def _cumf_kernel(f_ref, fb_ref, o_ref, *, chunk):
    s_len = f_ref.shape[0]
    row = lax.broadcasted_iota(jnp.int32, (chunk, chunk), 0)
    col = lax.broadcasted_iota(jnp.int32, (chunk, chunk), 1)
    tri = jnp.where(row >= col, 1.0, 0.0).astype(BF16)
    carry = jnp.zeros((1, LANES), F32)
    for ci in range(s_len // chunk):
        x = f_ref[ci * chunk:(ci + 1) * chunk, :] + fb_ref[...]
        ls = jnp.minimum(x, 0.0) - jnp.log(1.0 + jnp.exp(-jnp.abs(x)))
        hi = ls.astype(BF16)
        r1 = ls - hi.astype(F32)
        mid = r1.astype(BF16)
        lo = (r1 - mid.astype(F32)).astype(BF16)
        cs = _dot(tri, hi) + _dot(tri, mid) + _dot(tri, lo) + carry
        carry = cs[chunk - 1:chunk, :]
        o_ref[0, :, ci * chunk:(ci + 1) * chunk] = cs.T[0:SUBLANES, :]


def _cumf(fa, f_bias_pad, *, bsz, seq):
    chunk = 256 if seq % 256 == 0 else seq
    return pl.pallas_call(
        functools.partial(_cumf_kernel, chunk=chunk),
        grid=(bsz,),
        in_specs=[pl.BlockSpec((seq, LANES), lambda b: (b, 0)),
                  pl.BlockSpec((1, LANES), lambda b: (0, 0))],
        out_specs=pl.BlockSpec((1, SUBLANES, seq), lambda b: (b, 0, 0)),
        out_shape=jax.ShapeDtypeStruct((bsz, SUBLANES, seq), F32),
        compiler_params=_cparams(("parallel",), 8 * seq * LANES * 4),
        name="fox_cum_logf",
    )(fa, f_bias_pad)


def _fox_kernel(q_ref, k_ref, v_ref, f_ref, o_ref, *, tq, scale):
    qi = pl.program_id(2)
    q = (q_ref[0].astype(F32) * scale).astype(BF16)
    q0 = pl.multiple_of(qi * tq, tq)
    fq_row = f_ref[0, :, pl.ds(q0, tq)]
    fq = jnp.broadcast_to(fq_row, (LANES, tq)).T
    fq = jnp.concatenate([fq] * (tq // LANES), axis=1)

    def tile(j0, m, l, acc, diagonal):
        ks = k_ref[0, pl.ds(j0, tq), :]
        vs = v_ref[0, pl.ds(j0, tq), :]
        s = lax.dot_general(q, ks, (((1,), (1,)), ((), ())), preferred_element_type=F32)
        s = s + (fq - f_ref[0, :, pl.ds(j0, tq)])
        if diagonal:
            r = lax.broadcasted_iota(jnp.int32, (tq, tq), 0)
            c = lax.broadcasted_iota(jnp.int32, (tq, tq), 1)
            s = jnp.where(c <= r, s, NEG_BIG)
        m_new = jnp.maximum(m, jnp.max(s, axis=-1, keepdims=True))
        alpha = jnp.exp(m - m_new)
        p = jnp.exp(s - m_new)
        l = alpha * l + jnp.sum(p, axis=-1, keepdims=True)
        acc = alpha * acc + _dot(p.astype(BF16), vs)
        return m_new, l, acc

    def body(j, carry):
        return tile(pl.multiple_of(j * tq, tq), *carry, diagonal=False)

    init = (jnp.full((tq, 1), NEG_BIG, F32), jnp.zeros((tq, 1), F32), jnp.zeros((tq, HEAD_DIM), F32))
    m, l, acc = lax.fori_loop(0, qi, body, init)
    m, l, acc = tile(q0, m, l, acc, diagonal=True)
    o_ref[0] = (acc / l).astype(BF16)


def _fox(z3, frow, *, heads, q_blk, k_blk, v_blk, tq):
    bsz, seq, _ = z3.shape
    width = heads * HEAD_DIM
    return pl.pallas_call(
        functools.partial(_fox_kernel, tq=tq, scale=HEAD_DIM ** -0.5),
        grid=(bsz, heads, seq // tq),
        in_specs=[pl.BlockSpec((1, tq, HEAD_DIM), lambda b, h, i: (b, i, q_blk + h)),
                  pl.BlockSpec((1, seq, HEAD_DIM), lambda b, h, i: (b, 0, k_blk + h)),
                  pl.BlockSpec((1, seq, HEAD_DIM), lambda b, h, i: (b, 0, v_blk + h)),
                  pl.BlockSpec((1, 1, seq), lambda b, h, i: (b * SUBLANES + h, 0, 0))],
        out_specs=pl.BlockSpec((1, tq, HEAD_DIM), lambda b, h, i: (b, i, h)),
        out_shape=jax.ShapeDtypeStruct((bsz, seq, width), BF16),
        compiler_params=_cparams(("parallel", "parallel", "arbitrary"), 32 << 20),
        name="fox_attention",
    )(z3, z3, z3, frow)


def _ssm_disc_kernel(are_ref, aim_ref, ldt_ref, bre_ref, bim_ref, abr_ref, abi_ref, bbr_ref, bbi_ref):
    a_re = are_ref[...]
    a_im = aim_ref[...]
    dt = jnp.exp(ldt_ref[...])
    mag = jnp.exp(a_re * dt)
    ab_re = mag * jnp.cos(a_im * dt)
    ab_im = mag * jnp.sin(a_im * dt)
    den = a_re * a_re + a_im * a_im
    num_re = ab_re - 1.0
    coef_re = (num_re * a_re + ab_im * a_im) / den
    coef_im = (ab_im * a_re - num_re * a_im) / den
    b_re = bre_ref[...]
    b_im = bim_ref[...]
    abr_ref[...] = ab_re
    abi_ref[...] = ab_im
    bbr_ref[...] = coef_re * b_re - coef_im * b_im
    bbi_ref[...] = coef_re * b_im + coef_im * b_re


def _ssm_disc(a_re, a_im, log_dt, b_re, b_im):
    g, p, n = b_re.shape
    e = p * n
    expand = lambda a: jnp.repeat(a, n, axis=-1)
    args = (expand(a_re), expand(a_im), jnp.broadcast_to(log_dt[:, None], (g, e)),
            b_re.reshape(g, e), b_im.reshape(g, e))
    spec = pl.BlockSpec((g, e), lambda: (0, 0))
    outs = pl.pallas_call(
        _ssm_disc_kernel,
        in_specs=[spec] * 5,
        out_specs=[spec] * 4,
        out_shape=[jax.ShapeDtypeStruct((g, e), F32)] * 4,
        name="ssm_discretize",
    )(*args)
    ab_re, ab_im, bb_re, bb_im = outs
    return ab_re[:, ::n], ab_im[:, ::n], bb_re.reshape(g, p, n), bb_im.reshape(g, p, n)


def _gelu_tanh(x):
    return 0.5 * x * (1.0 + jnp.tanh(math.sqrt(2.0 / math.pi) * (x + 0.044715 * (x * x * x))))


def _ssm_kernel(u_ref, wb_ref, wc_ref, ar_ref, ai_ref, d_ref, o_ref, bu_ref, st_ref, *, tc, half):
    ti = pl.program_id(1)

    @pl.when(ti == 0)
    def _():
        st_ref[...] = jnp.zeros_like(st_ref)

    u = u_ref[...]
    bu_ref[...] = _dot(u, wb_ref[0])
    ar = ar_ref[0]
    ai = ai_ref[0]

    def step(t, carry):
        xr, xi = carry
        r = pl.multiple_of(t * SUBLANES, SUBLANES)
        bur = bu_ref[pl.ds(r, SUBLANES), 0:half]
        bui = bu_ref[pl.ds(r, SUBLANES), half:2 * half]
        nxr = ar * xr - ai * xi + bur
        nxi = ar * xi + ai * xr + bui
        bu_ref[pl.ds(r, SUBLANES), 0:half] = nxr
        bu_ref[pl.ds(r, SUBLANES), half:2 * half] = nxi
        return nxr, nxi

    xr, xi = lax.fori_loop(0, tc, step, (st_ref[:, 0:half], st_ref[:, half:2 * half]), unroll=8)
    st_ref[:, 0:half] = xr
    st_ref[:, half:2 * half] = xi
    y = _dot(bu_ref[...].astype(BF16), wc_ref[0])
    y = y + d_ref[...] * u.astype(F32)
    o_ref[...] = _gelu_tanh(y).astype(BF16)


def _ssm(u_tm, wb, wc, ar, ai, d_skip, *, seq, tc):
    rows, width = u_tm.shape
    n_chunk = width // LANES
    half = wb.shape[2] // 2
    rt = tc * SUBLANES
    return pl.pallas_call(
        functools.partial(_ssm_kernel, tc=tc, half=half),
        grid=(n_chunk, seq // tc),
        in_specs=[pl.BlockSpec((rt, LANES), lambda c, t: (t, c)),
                  pl.BlockSpec((1, LANES, 2 * half), lambda c, t: (c, 0, 0)),
                  pl.BlockSpec((1, 2 * half, LANES), lambda c, t: (c, 0, 0)),
                  pl.BlockSpec((1, SUBLANES, half), lambda c, t: (c, 0, 0)),
                  pl.BlockSpec((1, SUBLANES, half), lambda c, t: (c, 0, 0)),
                  pl.BlockSpec((1, LANES), lambda c, t: (0, c))],
        out_specs=pl.BlockSpec((rt, LANES), lambda c, t: (t, c)),
        out_shape=jax.ShapeDtypeStruct((rows, width), BF16),
        scratch_shapes=[pltpu.VMEM((rt, 2 * half), F32), pltpu.VMEM((SUBLANES, 2 * half), F32)],
        compiler_params=_cparams(("parallel", "arbitrary"), 3 * rt * 2 * half * 4 + (8 << 20)),
        name="s5_ssm",
    )(u_tm, wb, wc, ar, ai, d_skip.reshape(1, width))


def _ssm_weights(ab_re, ab_im, bb_re, bb_im, c_re, c_im):
    g, p, n = bb_re.shape
    gpc = LANES // n
    nch = g // gpc
    eye = jnp.eye(gpc, dtype=F32)
    bb = jnp.stack([bb_re, bb_im], axis=0).reshape(2, nch, gpc, p, n)
    wb = jnp.einsum('rcgpn,gh->cgnrhp', bb, eye).reshape(nch, gpc * n, 2 * gpc * p)
    cc = jnp.stack([c_re, -c_im], axis=0).reshape(2, nch, gpc, n, p)
    wc = jnp.einsum('rcgnp,gh->crhpgn', cc, eye).reshape(nch, 2 * gpc * p, gpc * n)
    ar = jnp.broadcast_to(ab_re.reshape(nch, 1, gpc * p), (nch, SUBLANES, gpc * p))
    ai = jnp.broadcast_to(ab_im.reshape(nch, 1, gpc * p), (nch, SUBLANES, gpc * p))
    return wb.astype(BF16), wc.astype(BF16), ar, ai


def _ret_kernel(q_ref, k_ref, v_ref, g_ref, cos_ref, sin_ref, intra_ref, qd_ref, kd_ref, cd_ref, gn_ref,
                o_ref, *, n_chunks, scale):
    half = HEAD_DIM // 2
    intra = intra_ref[0]
    qd = qd_ref[0]
    kd = kd_ref[0]
    cd = cd_ref[0, 0:1, :]
    gn = gn_ref[...]

    def rot(t, cs, sn):
        return t * cs + pltpu.roll(t, half, 1) * sn

    def body(ci, state):
        r0 = pl.multiple_of(ci * RET_CHUNK, RET_CHUNK)
        sl = pl.ds(r0, RET_CHUNK)
        cs = cos_ref[sl, :]
        sn = sin_ref[sl, :]
        qc = rot(q_ref[0, sl, :].astype(F32), cs, sn)
        kc = rot(k_ref[0, sl, :].astype(F32), cs, sn) * scale
        vc = v_ref[0, sl, :]
        scores = lax.dot_general(qc.astype(BF16), kc.astype(BF16), (((1,), (1,)), ((), ())),
                                 preferred_element_type=F32) * intra
        o = _dot(scores.astype(BF16), vc) + _dot((qc * qd).astype(BF16), state.astype(BF16))
        kv = lax.dot_general((kc * kd).astype(BF16), vc, (((0,), (0,)), ((), ())),
                             preferred_element_type=F32)
        state = cd * state + kv
        mu = jnp.mean(o, axis=-1, keepdims=True)
        oc = o - mu
        var = jnp.mean(oc * oc, axis=-1, keepdims=True)
        on = oc * lax.rsqrt(var + EPS)
        gr = g_ref[0, sl, :].astype(F32)
        o_ref[0, sl, :] = ((on * gn) * (gr * jax.nn.sigmoid(gr))).astype(BF16)
        return state

    lax.fori_loop(0, n_chunks, body, jnp.zeros((HEAD_DIM, HEAD_DIM), F32))


def _ret_tables(seq, heads):
    half = HEAD_DIM // 2
    pos = jnp.arange(seq, dtype=F32)
    inv_freq = 1.0 / (ROPE_BASE ** jnp.linspace(0.0, 1.0, half, dtype=F32))
    ang = pos[:, None] * inv_freq[None, :]
    cos = jnp.cos(ang)
    sin = jnp.sin(ang)
    cos2 = jnp.concatenate([cos, cos], axis=-1)
    sin2 = jnp.concatenate([-sin, sin], axis=-1)
    log_gamma = jnp.log(1.0 - jnp.exp2(-RET_DECAY_BASE - jnp.arange(heads, dtype=F32)))
    idx = jnp.arange(RET_CHUNK, dtype=F32)
    rel = idx[:, None] - idx[None, :]
    intra = jnp.where(rel[None] >= 0, jnp.exp(jnp.maximum(rel, 0.0)[None] * log_gamma[:, None, None]), 0.0)
    q_decay = jnp.exp((idx + 1.0)[None, :] * log_gamma[:, None])
    k_decay = jnp.exp((RET_CHUNK - 1.0 - idx)[None, :] * log_gamma[:, None])
    chunk_decay = jnp.exp(RET_CHUNK * log_gamma)
    qd = jnp.broadcast_to(q_decay[:, :, None], (heads, RET_CHUNK, HEAD_DIM))
    kd = jnp.broadcast_to(k_decay[:, :, None], (heads, RET_CHUNK, HEAD_DIM))
    cdec = jnp.broadcast_to(chunk_decay[:, None, None], (heads, SUBLANES, HEAD_DIM))
    return cos2, sin2, intra, qd, kd, cdec


def _ret(z3, tables, gn_w, *, heads, q_blk, k_blk, v_blk, g_blk):
    bsz, seq, _ = z3.shape
    width = heads * HEAD_DIM
    cos2, sin2, intra, qd, kd, cdec = tables
    head_spec = lambda blk: pl.BlockSpec((1, seq, HEAD_DIM), lambda b, h: (b, 0, blk + h))
    tab_spec = pl.BlockSpec((seq, HEAD_DIM), lambda b, h: (0, 0))
    per_head = lambda rows: pl.BlockSpec((1, rows, HEAD_DIM), lambda b, h: (h, 0, 0))
    return pl.pallas_call(
        functools.partial(_ret_kernel, n_chunks=seq // RET_CHUNK, scale=HEAD_DIM ** -0.5),
        grid=(bsz, heads),
        in_specs=[head_spec(q_blk), head_spec(k_blk), head_spec(v_blk), head_spec(g_blk),
                  tab_spec, tab_spec, per_head(RET_CHUNK), per_head(RET_CHUNK), per_head(RET_CHUNK),
                  per_head(SUBLANES), pl.BlockSpec((1, HEAD_DIM), lambda b, h: (0, h))],
        out_specs=pl.BlockSpec((1, seq, HEAD_DIM), lambda b, h: (b, 0, h)),
        out_shape=jax.ShapeDtypeStruct((bsz, seq, width), BF16),
        compiler_params=_cparams(("parallel", "arbitrary"), 32 << 20),
        name="retention",
    )(z3, z3, z3, z3, cos2, sin2, intra, qd, kd, cdec, gn_w.reshape(1, width))


def _merge_kernel(yf_ref, yg_ref, yr_ref, g0_ref, g1_ref, g2_ref, wb_ref, bg_ref, gw_ref, gb_ref,
                  o_ref, ys_ref):
    j = pl.program_id(1)

    @pl.when(j == 0)
    def _():
        yg = yg_ref[...]
        t = _dot(yg, gw_ref[...]) + gb_ref[...]
        ys_ref[...] = (yg.astype(F32) * jax.nn.sigmoid(t)).astype(BF16)

    def gate(g_ref, k):
        return jax.nn.sigmoid(g_ref[...].astype(F32) + bg_ref[k:k + 1, :])

    acc = gate(g0_ref, 0) * _dot(yf_ref[...], wb_ref[0])
    acc += gate(g1_ref, 1) * _dot(ys_ref[...], wb_ref[1])
    acc += gate(g2_ref, 2) * _dot(yr_ref[...], wb_ref[2])
    o_ref[...] = acc.astype(BF16)


def _merge(y_fox, y_g, y_ret, z, w_branch, b_gate, glu_w, glu_b, *, gate_blk, tm, tn):
    t, width = y_fox.shape
    d = w_branch.shape[2]
    y_spec = pl.BlockSpec((tm, width), lambda i, j: (i, 0))
    nb = d // tn
    g_spec = lambda k: pl.BlockSpec((tm, tn), lambda i, j: (i, gate_blk * LANES // tn + k * nb + j))
    vmem = 3 * 2 * tm * width * 2 + 3 * 2 * tm * tn * 2 + 2 * 3 * width * tn * 2 + 2 * width * width * 2 \
        + tm * width * 2 + 2 * tm * tn * 2 + 6 * tm * tn * 4 + 2 * tm * width * 4
    return pl.pallas_call(
        _merge_kernel,
        grid=(t // tm, nb),
        in_specs=[y_spec, y_spec, y_spec, g_spec(0), g_spec(1), g_spec(2),
                  pl.BlockSpec((3, width, tn), lambda i, j: (0, 0, j)),
                  pl.BlockSpec((3, tn), lambda i, j: (0, j)),
                  pl.BlockSpec((width, width), lambda i, j: (0, 0)),
                  pl.BlockSpec((1, width), lambda i, j: (0, 0))],
        out_specs=pl.BlockSpec((tm, tn), lambda i, j: (i, j)),
        out_shape=jax.ShapeDtypeStruct((t, d), BF16),
        scratch_shapes=[pltpu.VMEM((tm, width), BF16)],
        compiler_params=_cparams(("parallel", "arbitrary"), vmem),
        name="branch_merge",
    )(y_fox, y_g, y_ret, z, z, z, w_branch, b_gate, glu_w, glu_b.reshape(1, width))


def _outproj_kernel(m_ref, w_ref, x_ref, mod_ref, o_ref, *, gate_row):
    y = _dot(m_ref[...], w_ref[...])
    o_ref[...] = x_ref[...] + mod_ref[0, gate_row:gate_row + 1, :] * y


def _outproj(merged, w_out, x2, mod3, *, seq, gate_row, tm, tn):
    t, d = x2.shape
    per_b = seq // tm
    vmem = 2 * tm * d * 2 + 2 * d * tn * 2 + 4 * tm * tn * 4 + 2 * tm * tn * 4
    return pl.pallas_call(
        functools.partial(_outproj_kernel, gate_row=gate_row),
        grid=(t // tm, d // tn),
        in_specs=[pl.BlockSpec((tm, d), lambda i, j: (i, 0)),
                  pl.BlockSpec((d, tn), lambda i, j: (0, j)),
                  pl.BlockSpec((tm, tn), lambda i, j: (i, j)),
                  pl.BlockSpec((1, N_MOD, tn), lambda i, j: (i // per_b, 0, j))],
        out_specs=pl.BlockSpec((tm, tn), lambda i, j: (i, j)),
        out_shape=jax.ShapeDtypeStruct((t, d), F32),
        compiler_params=_cparams(("parallel", "arbitrary"), vmem),
        name="out_proj",
    )(merged, w_out, x2, mod3)


def _tiles(seq, d, f):
    tm = min(1024, seq)
    tf = 512
    fp = -(-f // tf) * tf
    return dict(tm=tm, tm_ffn=min(512, seq), tf=tf, fp=fp, tn=min(1024, d), tq=min(256, seq), tc=min(256, seq))


def kernel(x, c, ada_w, ada_b, norm_w, final_norm_w, ffn_w1, ffn_w3, ffn_w2, w_in, fox_f_bias, ssm_A_re, ssm_A_im, ssm_log_dt, ssm_B_re, ssm_B_im, ssm_C_re, ssm_C_im, ssm_D, glu_w, glu_b, ret_gn_w, w_branch, b_gate, w_out):
    bsz, seq, d = x.shape
    depth = ada_w.shape[0]
    width = d // 2
    heads = width // HEAD_DIM
    f = ffn_w1.shape[-1]
    assert bsz == SUBLANES, "the S5 kernel maps the batch onto the 8 sublanes of a vreg"
    assert heads <= SUBLANES and width % LANES == 0 and seq % RET_CHUNK == 0
    cfg = _tiles(seq, d, f)
    tm, tf, fp, tn = cfg["tm"], cfg["tf"], cfg["fp"], cfg["tn"]
    t = bsz * seq

    pad_f = fp - f
    w1 = jnp.pad(ffn_w1, ((0, 0), (0, 0), (0, 0), (0, pad_f))).astype(BF16)
    w3 = jnp.pad(ffn_w3, ((0, 0), (0, 0), (0, 0), (0, pad_f))).astype(BF16)
    w2 = jnp.pad(ffn_w2, ((0, 0), (0, 0), (0, pad_f), (0, 0))).astype(BF16)
    f_lo = 3 * width
    f_hi = f_lo + heads
    w_main = jnp.concatenate([w_in[:, :, :f_lo], w_in[:, :, f_hi:]], axis=-1).astype(BF16)
    w_f = jnp.pad(w_in[:, :, f_lo:f_hi], ((0, 0), (0, 0), (0, LANES - heads))).astype(BF16)
    f_bias = jnp.pad(fox_f_bias, ((0, 0), (0, LANES - heads)))
    glu_wb = glu_w.astype(BF16)
    w_br = w_branch.astype(BF16)
    w_o = w_out.astype(BF16)
    wblk = width // LANES
    blk = dict(qa=0, ka=wblk, va=2 * wblk, us=3 * wblk, qr=4 * wblk, kr=5 * wblk, vr=6 * wblk,
               gr=7 * wblk, gl=8 * wblk)
    ret_tables = _ret_tables(seq, heads)

    x2 = x.reshape(t, d)
    for l in range(depth):
        mod3 = _ada(c, ada_w[l], ada_b[l]).reshape(bsz, N_MOD, d)
        x2 = _ffn(x2, mod3, norm_w[l, 0], w1[l, 0], w3[l, 0], w2[l, 0], None,
                  seq=seq, mod_base=0, tm=cfg["tm_ffn"], tf=tf)
        z, fa = _proj(x2, mod3, norm_w[l, 1], w_main[l], w_f[l], seq=seq, mod_base=3, tm=tm, tn=tn)
        z3 = z.reshape(bsz, seq, z.shape[1])
        cum_f = _cumf(fa, f_bias[l:l + 1], bsz=bsz, seq=seq)
        y_fox = _fox(z3, cum_f.reshape(bsz * SUBLANES, 1, seq), heads=heads,
                     q_blk=blk["qa"], k_blk=blk["ka"], v_blk=blk["va"], tq=cfg["tq"])
        ab_re, ab_im, bb_re, bb_im = _ssm_disc(ssm_A_re[l], ssm_A_im[l], ssm_log_dt[l], ssm_B_re[l], ssm_B_im[l])
        wb, wc, ar, ai = _ssm_weights(ab_re, ab_im, bb_re, bb_im, ssm_C_re[l], ssm_C_im[l])
        u_tm = jnp.transpose(z3[:, :, blk["us"] * LANES:(blk["us"] + wblk) * LANES], (1, 0, 2)).reshape(t, width)
        y_tm = _ssm(u_tm, wb, wc, ar, ai, ssm_D[l], seq=seq, tc=cfg["tc"])
        y_g = jnp.transpose(y_tm.reshape(seq, bsz, width), (1, 0, 2)).reshape(t, width)
        y_ret = _ret(z3, ret_tables, ret_gn_w[l], heads=heads,
                     q_blk=blk["qr"], k_blk=blk["kr"], v_blk=blk["vr"], g_blk=blk["gr"])
        merged = _merge(y_fox.reshape(t, width), y_g, y_ret.reshape(t, width), z, w_br[l], b_gate[l],
                        glu_wb[l], glu_b[l], gate_blk=blk["gl"], tm=tm, tn=min(512, d))
        x2 = _outproj(merged, w_o[l], x2, mod3, seq=seq, gate_row=5, tm=tm, tn=tn)
        x2 = _ffn(x2, mod3, norm_w[l, 2], w1[l, 1], w3[l, 1], w2[l, 1],
                  final_norm_w if l == depth - 1 else None, seq=seq, mod_base=6, tm=cfg["tm_ffn"], tf=tf)
    return x2.reshape(bsz, seq, d)
```

```python
import functools
import math

import jax
import jax.numpy as jnp
from jax import lax
from jax.experimental import pallas as pl
from jax.experimental.pallas import tpu as pltpu

F32 = jnp.float32
BF16 = jnp.bfloat16

EPS = 1e-6
HEAD_DIM = 128
SSM_GROUP = 16
SSM_STATE = 64
N_MOD = 9
RET_CHUNK = 128
RET_DECAY_BASE = 5.0
ROPE_BASE = 10000.0
LANES = 128
SUBLANES = 8
V7X_VMEM_BUDGET = 56 * 1024 * 1024
NEG_BIG = -1e30
LOG2E = math.log2(math.e)


def _cparams(semantics, vmem_bytes):
    vmem = int(min(max(vmem_bytes, 16 * 1024 * 1024), V7X_VMEM_BUDGET))
    return pltpu.CompilerParams(dimension_semantics=semantics, vmem_limit_bytes=vmem)


def _dot(a, b):
    return jnp.dot(a, b, preferred_element_type=F32)


def _norm_mod(x, nw, sh, sc):
    ms = jnp.mean(x * x, axis=-1, keepdims=True)
    return (x * lax.rsqrt(ms + EPS)) * nw * (1.0 + sc) + sh


def _ada_kernel(c_ref, w_ref, b_ref, o_ref):
    c = c_ref[...]
    ca = (c * jax.nn.sigmoid(c)).astype(BF16)
    o_ref[...] = _dot(ca, w_ref[...].astype(BF16)) + b_ref[...]


def _ada(c, w, b):
    bsz, d = c.shape
    n = w.shape[1]
    tn = 1024 if n % 1024 == 0 else n
    return pl.pallas_call(
        _ada_kernel,
        grid=(n // tn,),
        in_specs=[pl.BlockSpec((bsz, d), lambda j: (0, 0)),
                  pl.BlockSpec((d, tn), lambda j: (0, j)),
                  pl.BlockSpec((1, tn), lambda j: (0, j))],
        out_specs=pl.BlockSpec((bsz, tn), lambda j: (0, j)),
        out_shape=jax.ShapeDtypeStruct((bsz, n), F32),
        compiler_params=_cparams(("arbitrary",), 4 * d * tn * 4),
        name="ada_mod",
    )(c, w, b.reshape(1, n))


def _ffn_kernel(x_ref, mod_ref, nw_ref, w1_ref, w3_ref, w2_ref, *rest, mod_base, final):
    if final:
        fw_ref, o_ref, h_ref = rest
    else:
        o_ref, h_ref = rest
    j = pl.program_id(1)

    @pl.when(j == 0)
    def _():
        sh = mod_ref[0, mod_base:mod_base + 1, :]
        sc = mod_ref[0, mod_base + 1:mod_base + 2, :]
        h_ref[...] = _norm_mod(x_ref[...], nw_ref[...], sh, sc).astype(BF16)
        o_ref[...] = jnp.zeros_like(o_ref)

    h = h_ref[...]
    a = _dot(h, w1_ref[...])
    b = _dot(h, w3_ref[...])
    g = (a * jax.nn.sigmoid(a) * b).astype(BF16)
    o_ref[...] += _dot(g, w2_ref[...])

    @pl.when(j == pl.num_programs(1) - 1)
    def _():
        gate = mod_ref[0, mod_base + 2:mod_base + 3, :]
        y = x_ref[...] + 0.5 * gate * o_ref[...]
        if final:
            ms = jnp.mean(y * y, axis=-1, keepdims=True)
            y = y * lax.rsqrt(ms + EPS) * fw_ref[...]
        o_ref[...] = y


def _ffn(x2, mod3, nw, w1, w3, w2, final_w, *, seq, mod_base, tm, tf):
    t, d = x2.shape
    fp = w1.shape[1]
    per_b = seq // tm
    final = final_w is not None
    in_specs = [pl.BlockSpec((tm, d), lambda i, j: (i, 0)),
                pl.BlockSpec((1, N_MOD, d), lambda i, j: (i // per_b, 0, 0)),
                pl.BlockSpec((1, d), lambda i, j: (0, 0)),
                pl.BlockSpec((d, tf), lambda i, j: (0, j)),
                pl.BlockSpec((d, tf), lambda i, j: (0, j)),
                pl.BlockSpec((tf, d), lambda i, j: (j, 0))]
    args = [x2, mod3, nw.reshape(1, d), w1, w3, w2]
    if final:
        in_specs.append(pl.BlockSpec((1, d), lambda i, j: (0, 0)))
        args.append(final_w.reshape(1, d))
    vmem = 4 * tm * d * 4 + tm * d * 2 + 6 * d * tf * 2 + 3 * tm * tf * 4 + 2 * tm * d * 4
    return pl.pallas_call(
        functools.partial(_ffn_kernel, mod_base=mod_base, final=final),
        grid=(t // tm, fp // tf),
        in_specs=in_specs,
        out_specs=pl.BlockSpec((tm, d), lambda i, j: (i, 0)),
        out_shape=jax.ShapeDtypeStruct((t, d), F32),
        scratch_shapes=[pltpu.VMEM((tm, d), BF16)],
        compiler_params=_cparams(("parallel", "arbitrary"), vmem),
        name="ffn_final" if final else "ffn",
    )(*args)


def _proj_kernel(x_ref, mod_ref, nw_ref, w_ref, wf_ref, z_ref, f_ref, h_ref, *, mod_base):
    j = pl.program_id(1)

    @pl.when(j == 0)
    def _():
        sh = mod_ref[0, mod_base:mod_base + 1, :]
        sc = mod_ref[0, mod_base + 1:mod_base + 2, :]
        h = _norm_mod(x_ref[...], nw_ref[...], sh, sc).astype(BF16)
        h_ref[...] = h
        f_ref[...] = _dot(h, wf_ref[...])

    z_ref[...] = _dot(h_ref[...], w_ref[...]).astype(BF16)


def _proj(x2, mod3, nw, w_main, w_f, *, seq, mod_base, tm, tn):
    t, d = x2.shape
    nz = w_main.shape[1]
    per_b = seq // tm
    vmem = 2 * tm * d * 4 + tm * d * 2 + 2 * d * tn * 2 + 2 * tm * tn * 2 + 2 * (tm * LANES * 4 + d * LANES * 2) \
        + 2 * tm * tn * 4 + tm * d * 4
    return pl.pallas_call(
        functools.partial(_proj_kernel, mod_base=mod_base),
        grid=(t // tm, nz // tn),
        in_specs=[pl.BlockSpec((tm, d), lambda i, j: (i, 0)),
                  pl.BlockSpec((1, N_MOD, d), lambda i, j: (i // per_b, 0, 0)),
                  pl.BlockSpec((1, d), lambda i, j: (0, 0)),
                  pl.BlockSpec((d, tn), lambda i, j: (0, j)),
                  pl.BlockSpec((d, LANES), lambda i, j: (0, 0))],
        out_specs=[pl.BlockSpec((tm, tn), lambda i, j: (i, j)),
                   pl.BlockSpec((tm, LANES), lambda i, j: (i, 0))],
        out_shape=[jax.ShapeDtypeStruct((t, nz), BF16),
                   jax.ShapeDtypeStruct((t, LANES), F32)],
        scratch_shapes=[pltpu.VMEM((tm, d), BF16)],
        compiler_params=_cparams(("parallel", "arbitrary"), vmem),
        name="in_proj",
    )(x2, mod3, nw.reshape(1, d), w_main, w_f)


def _split3(x):
    hi = x.astype(BF16)
    r1 = x - hi.astype(F32)
    mid = r1.astype(BF16)
    lo = (r1 - mid.astype(F32)).astype(BF16)
    return hi, mid, lo


def _cumf_kernel(f_ref, fb_ref, eq_ref, ek_ref, cq_ref, ck_ref, qa_ref, ka_ref, *, chunk):
    s_len = f_ref.shape[0]
    row = lax.broadcasted_iota(jnp.int32, (chunk, chunk), 0)
    col = lax.broadcasted_iota(jnp.int32, (chunk, chunk), 1)
    tri = jnp.where(row >= col, 1.0, 0.0).astype(BF16)
    carry = jnp.zeros((1, LANES), F32)
    for ci in range(s_len // chunk):
        rows = slice(ci * chunk, (ci + 1) * chunk)
        x = f_ref[rows, :] + fb_ref[...]
        ls = jnp.minimum(x, 0.0) - jnp.log(1.0 + jnp.exp(-jnp.abs(x)))
        hi, mid, lo = _split3(ls)
        cs = _dot(tri, hi) + _dot(tri, mid) + _dot(tri, lo) + carry
        carry = cs[chunk - 1:chunk, :]
        pieces = jnp.concatenate(_split3(cs * LOG2E), axis=1)
        qa_ref[0, rows, :] = (_dot(pieces, eq_ref[...]) + cq_ref[...]).astype(BF16)
        ka_ref[0, rows, :] = (_dot(pieces, ek_ref[...]) + ck_ref[...]).astype(BF16)


def _fox_aug_tables(heads):
    n = heads * HEAD_DIM
    eq = jnp.zeros((3 * LANES, n), F32)
    ek = jnp.zeros((3 * LANES, n), F32)
    cq = jnp.zeros((1, n), F32)
    ck = jnp.zeros((1, n), F32)
    h = jnp.arange(heads)
    for piece in range(3):
        eq = eq.at[piece * LANES + h, h * HEAD_DIM + piece].set(1.0)
        ek = ek.at[piece * LANES + h, h * HEAD_DIM + 3 + piece].set(-1.0)
        cq = cq.at[0, h * HEAD_DIM + 3 + piece].set(1.0)
        ck = ck.at[0, h * HEAD_DIM + piece].set(1.0)
    return eq.astype(BF16), ek.astype(BF16), cq, ck


def _cumf(fa, f_bias_pad, aug_tables, *, bsz, seq, heads):
    chunk = 256 if seq % 256 == 0 else seq
    n = heads * HEAD_DIM
    eq, ek, cq, ck = aug_tables
    e_spec = pl.BlockSpec((3 * LANES, n), lambda b: (0, 0))
    c_spec = pl.BlockSpec((1, n), lambda b: (0, 0))
    o_spec = pl.BlockSpec((1, seq, n), lambda b: (b, 0, 0))
    return pl.pallas_call(
        functools.partial(_cumf_kernel, chunk=chunk),
        grid=(bsz,),
        in_specs=[pl.BlockSpec((seq, LANES), lambda b: (b, 0)),
                  pl.BlockSpec((1, LANES), lambda b: (0, 0)), e_spec, e_spec, c_spec, c_spec],
        out_specs=[o_spec, o_spec],
        out_shape=[jax.ShapeDtypeStruct((bsz, seq, n), BF16)] * 2,
        compiler_params=_cparams(("parallel",), 8 * seq * n * 2 + 8 * seq * LANES * 4),
        name="fox_cum_logf",
    )(fa, f_bias_pad, eq, ek, cq, ck)


def _fox_kernel(q_ref, k_ref, v_ref, qa_ref, ka_ref, o_ref, *, tile, scale):
    seq = q_ref.shape[1]
    qs = (q_ref[0].astype(F32) * (scale * LOG2E)).astype(BF16)
    qp = jnp.concatenate([qs, qa_ref[0]], axis=1)
    kp = jnp.concatenate([k_ref[0], ka_ref[0]], axis=1)
    r = lax.broadcasted_iota(jnp.int32, (tile, tile), 0)
    c = lax.broadcasted_iota(jnp.int32, (tile, tile), 1)
    causal = c <= r
    for i in range(seq // tile):
        qi = qp[i * tile:(i + 1) * tile]
        m = l = acc = None
        for j in range(i + 1):
            keys = slice(j * tile, (j + 1) * tile)
            s = lax.dot_general(qi, kp[keys], (((1,), (1,)), ((), ())), preferred_element_type=F32)
            if j == i:
                s = jnp.where(causal, s, NEG_BIG)
            m_tile = jnp.max(s, axis=-1, keepdims=True)
            if j == 0:
                m = m_tile
                p = jnp.exp2(s - m)
                l = jnp.sum(p, axis=-1, keepdims=True)
                acc = _dot(p.astype(BF16), v_ref[0, keys, :])
            else:
                m_new = jnp.maximum(m, m_tile)
                alpha = jnp.exp2(m - m_new)
                p = jnp.exp2(s - m_new)
                l = alpha * l + jnp.sum(p, axis=-1, keepdims=True)
                acc = alpha * acc + _dot(p.astype(BF16), v_ref[0, keys, :])
                m = m_new
        o_ref[0, i * tile:(i + 1) * tile, :] = (acc / l).astype(BF16)


def _fox(z3, q_aug, k_aug, *, heads, q_blk, k_blk, v_blk, tile):
    bsz, seq, _ = z3.shape
    width = heads * HEAD_DIM
    head_spec = lambda blk: pl.BlockSpec((1, seq, HEAD_DIM), lambda b, h: (b, 0, blk + h))
    return pl.pallas_call(
        functools.partial(_fox_kernel, tile=tile, scale=HEAD_DIM ** -0.5),
        grid=(bsz, heads),
        in_specs=[head_spec(q_blk), head_spec(k_blk), head_spec(v_blk), head_spec(0), head_spec(0)],
        out_specs=head_spec(0),
        out_shape=jax.ShapeDtypeStruct((bsz, seq, width), BF16),
        compiler_params=_cparams(("parallel", "parallel"), 40 << 20),
        name="fox_attention",
    )(z3, z3, z3, q_aug, k_aug)


def _ssm_disc_kernel(are_ref, aim_ref, ldt_ref, bre_ref, bim_ref, abr_ref, abi_ref, bbr_ref, bbi_ref):
    a_re = are_ref[...]
    a_im = aim_ref[...]
    dt = jnp.exp(ldt_ref[...])
    mag = jnp.exp(a_re * dt)
    ab_re = mag * jnp.cos(a_im * dt)
    ab_im = mag * jnp.sin(a_im * dt)
    den = a_re * a_re + a_im * a_im
    num_re = ab_re - 1.0
    coef_re = (num_re * a_re + ab_im * a_im) / den
    coef_im = (ab_im * a_re - num_re * a_im) / den
    b_re = bre_ref[...]
    b_im = bim_ref[...]
    abr_ref[...] = ab_re
    abi_ref[...] = ab_im
    bbr_ref[...] = coef_re * b_re - coef_im * b_im
    bbi_ref[...] = coef_re * b_im + coef_im * b_re


def _ssm_disc(a_re, a_im, log_dt, b_re, b_im):
    g, p, n = b_re.shape
    e = p * n
    expand = lambda a: jnp.repeat(a, n, axis=-1)
    args = (expand(a_re), expand(a_im), jnp.broadcast_to(log_dt[:, None], (g, e)),
            b_re.reshape(g, e), b_im.reshape(g, e))
    spec = pl.BlockSpec((g, e), lambda: (0, 0))
    outs = pl.pallas_call(
        _ssm_disc_kernel,
        in_specs=[spec] * 5,
        out_specs=[spec] * 4,
        out_shape=[jax.ShapeDtypeStruct((g, e), F32)] * 4,
        name="ssm_discretize",
    )(*args)
    ab_re, ab_im, bb_re, bb_im = outs
    return ab_re[:, ::n], ab_im[:, ::n], bb_re.reshape(g, p, n), bb_im.reshape(g, p, n)


def _gelu_tanh(x):
    return 0.5 * x * (1.0 + jnp.tanh(math.sqrt(2.0 / math.pi) * (x + 0.044715 * (x * x * x))))


def _ssm_kernel(u_ref, wb_ref, wc_ref, ar_ref, ai_ref, d_ref, o_ref, bu_ref, st_ref, *, tc, half):
    ti = pl.program_id(1)

    @pl.when(ti == 0)
    def _():
        st_ref[...] = jnp.zeros_like(st_ref)

    u = u_ref[...]
    bu_ref[...] = _dot(u, wb_ref[0])
    ar = ar_ref[0]
    ai = ai_ref[0]

    def step(t, carry):
        xr, xi = carry
        r = pl.multiple_of(t * SUBLANES, SUBLANES)
        bur = bu_ref[pl.ds(r, SUBLANES), 0:half]
        bui = bu_ref[pl.ds(r, SUBLANES), half:2 * half]
        nxr = ar * xr - ai * xi + bur
        nxi = ar * xi + ai * xr + bui
        bu_ref[pl.ds(r, SUBLANES), 0:half] = nxr
        bu_ref[pl.ds(r, SUBLANES), half:2 * half] = nxi
        return nxr, nxi

    xr, xi = lax.fori_loop(0, tc, step, (st_ref[:, 0:half], st_ref[:, half:2 * half]), unroll=8)
    st_ref[:, 0:half] = xr
    st_ref[:, half:2 * half] = xi
    y = _dot(bu_ref[...].astype(BF16), wc_ref[0])
    y = y + d_ref[...] * u.astype(F32)
    o_ref[...] = _gelu_tanh(y).astype(BF16)


def _ssm(u_tm, wb, wc, ar, ai, d_skip, *, seq, tc):
    rows, width = u_tm.shape
    n_chunk = width // LANES
    half = wb.shape[2] // 2
    rt = tc * SUBLANES
    return pl.pallas_call(
        functools.partial(_ssm_kernel, tc=tc, half=half),
        grid=(n_chunk, seq // tc),
        in_specs=[pl.BlockSpec((rt, LANES), lambda c, t: (t, c)),
                  pl.BlockSpec((1, LANES, 2 * half), lambda c, t: (c, 0, 0)),
                  pl.BlockSpec((1, 2 * half, LANES), lambda c, t: (c, 0, 0)),
                  pl.BlockSpec((1, SUBLANES, half), lambda c, t: (c, 0, 0)),
                  pl.BlockSpec((1, SUBLANES, half), lambda c, t: (c, 0, 0)),
                  pl.BlockSpec((1, LANES), lambda c, t: (0, c))],
        out_specs=pl.BlockSpec((rt, LANES), lambda c, t: (t, c)),
        out_shape=jax.ShapeDtypeStruct((rows, width), BF16),
        scratch_shapes=[pltpu.VMEM((rt, 2 * half), F32), pltpu.VMEM((SUBLANES, 2 * half), F32)],
        compiler_params=_cparams(("parallel", "arbitrary"), 3 * rt * 2 * half * 4 + (8 << 20)),
        name="s5_ssm",
    )(u_tm, wb, wc, ar, ai, d_skip.reshape(1, width))


def _ssm_weights(ab_re, ab_im, bb_re, bb_im, c_re, c_im):
    g, p, n = bb_re.shape
    gpc = LANES // n
    nch = g // gpc
    eye = jnp.eye(gpc, dtype=F32)
    bb = jnp.stack([bb_re, bb_im], axis=0).reshape(2, nch, gpc, p, n)
    wb = jnp.einsum('rcgpn,gh->cgnrhp', bb, eye).reshape(nch, gpc * n, 2 * gpc * p)
    cc = jnp.stack([c_re, -c_im], axis=0).reshape(2, nch, gpc, n, p)
    wc = jnp.einsum('rcgnp,gh->crhpgn', cc, eye).reshape(nch, 2 * gpc * p, gpc * n)
    ar = jnp.broadcast_to(ab_re.reshape(nch, 1, gpc * p), (nch, SUBLANES, gpc * p))
    ai = jnp.broadcast_to(ab_im.reshape(nch, 1, gpc * p), (nch, SUBLANES, gpc * p))
    return wb.astype(BF16), wc.astype(BF16), ar, ai


def _ret_kernel(q_ref, k_ref, v_ref, g_ref, cos_ref, sin_ref, intra_ref, qd_ref, kd_ref, cd_ref, gn_ref,
                o_ref, *, chunk, scale):
    seq = q_ref.shape[1]
    half = HEAD_DIM // 2
    cs = cos_ref[...]
    sn = sin_ref[...]

    def rot(t):
        return t * cs + pltpu.roll(t, half, 1) * sn

    qr = rot(q_ref[0].astype(F32))
    kr = rot(k_ref[0].astype(F32)) * scale
    qb = qr.astype(BF16)
    kb = kr.astype(BF16)
    intra = intra_ref[0]
    qd = qd_ref[0]
    kd = kd_ref[0]
    cd = cd_ref[0, 0:1, :]
    gn = gn_ref[...]
    n_chunks = seq // chunk
    state = None
    for ci in range(n_chunks):
        rows = slice(ci * chunk, (ci + 1) * chunk)
        vc = v_ref[0, rows, :]
        scores = lax.dot_general(qb[rows], kb[rows], (((1,), (1,)), ((), ())),
                                 preferred_element_type=F32) * intra
        o = _dot(scores.astype(BF16), vc)
        if state is not None:
            o = o + _dot((qr[rows] * qd).astype(BF16), state.astype(BF16))
        if ci + 1 < n_chunks:
            kv = lax.dot_general((kr[rows] * kd).astype(BF16), vc, (((0,), (0,)), ((), ())),
                                 preferred_element_type=F32)
            state = kv if state is None else cd * state + kv
        mu = jnp.mean(o, axis=-1, keepdims=True)
        oc = o - mu
        var = jnp.mean(oc * oc, axis=-1, keepdims=True)
        on = oc * lax.rsqrt(var + EPS)
        gr = g_ref[0, rows, :].astype(F32)
        o_ref[0, rows, :] = ((on * gn) * (gr * jax.nn.sigmoid(gr))).astype(BF16)


def _ret_tables(seq, heads, chunk):
    half = HEAD_DIM // 2
    pos = jnp.arange(seq, dtype=F32)
    inv_freq = 1.0 / (ROPE_BASE ** jnp.linspace(0.0, 1.0, half, dtype=F32))
    ang = pos[:, None] * inv_freq[None, :]
    cos = jnp.cos(ang)
    sin = jnp.sin(ang)
    cos2 = jnp.concatenate([cos, cos], axis=-1)
    sin2 = jnp.concatenate([-sin, sin], axis=-1)
    log_gamma = jnp.log(1.0 - jnp.exp2(-RET_DECAY_BASE - jnp.arange(heads, dtype=F32)))
    idx = jnp.arange(chunk, dtype=F32)
    rel = idx[:, None] - idx[None, :]
    intra = jnp.where(rel[None] >= 0, jnp.exp(jnp.maximum(rel, 0.0)[None] * log_gamma[:, None, None]), 0.0)
    q_decay = jnp.exp((idx + 1.0)[None, :] * log_gamma[:, None])
    k_decay = jnp.exp((chunk - 1.0 - idx)[None, :] * log_gamma[:, None])
    chunk_decay = jnp.exp(chunk * log_gamma)
    qd = jnp.broadcast_to(q_decay[:, :, None], (heads, chunk, HEAD_DIM))
    kd = jnp.broadcast_to(k_decay[:, :, None], (heads, chunk, HEAD_DIM))
    cdec = jnp.broadcast_to(chunk_decay[:, None, None], (heads, SUBLANES, HEAD_DIM))
    return cos2, sin2, intra, qd, kd, cdec


def _ret(z3, tables, gn_w, *, heads, q_blk, k_blk, v_blk, g_blk, chunk):
    bsz, seq, _ = z3.shape
    width = heads * HEAD_DIM
    cos2, sin2, intra, qd, kd, cdec = tables
    head_spec = lambda blk: pl.BlockSpec((1, seq, HEAD_DIM), lambda h, b: (b, 0, blk + h))
    tab_spec = pl.BlockSpec((seq, HEAD_DIM), lambda h, b: (0, 0))
    per_head = lambda rows, cols: pl.BlockSpec((1, rows, cols), lambda h, b: (h, 0, 0))
    return pl.pallas_call(
        functools.partial(_ret_kernel, chunk=chunk, scale=HEAD_DIM ** -0.5),
        grid=(heads, bsz),
        in_specs=[head_spec(q_blk), head_spec(k_blk), head_spec(v_blk), head_spec(g_blk),
                  tab_spec, tab_spec, per_head(chunk, chunk), per_head(chunk, HEAD_DIM),
                  per_head(chunk, HEAD_DIM), per_head(SUBLANES, HEAD_DIM),
                  pl.BlockSpec((1, HEAD_DIM), lambda h, b: (0, h))],
        out_specs=pl.BlockSpec((1, seq, HEAD_DIM), lambda h, b: (b, 0, h)),
        out_shape=jax.ShapeDtypeStruct((bsz, seq, width), BF16),
        compiler_params=_cparams(("parallel", "parallel"), 40 << 20),
        name="retention",
    )(z3, z3, z3, z3, cos2, sin2, intra, qd, kd, cdec, gn_w.reshape(1, width))


def _merge_kernel(yf_ref, yg_ref, yr_ref, g0_ref, g1_ref, g2_ref, wb_ref, bg_ref, gw_ref, gb_ref,
                  o_ref, ys_ref):
    j = pl.program_id(1)

    @pl.when(j == 0)
    def _():
        yg = yg_ref[...]
        t = _dot(yg, gw_ref[...]) + gb_ref[...]
        ys_ref[...] = (yg.astype(F32) * jax.nn.sigmoid(t)).astype(BF16)

    def gate(g_ref, k):
        return jax.nn.sigmoid(g_ref[...].astype(F32) + bg_ref[k:k + 1, :])

    acc = gate(g0_ref, 0) * _dot(yf_ref[...], wb_ref[0])
    acc += gate(g1_ref, 1) * _dot(ys_ref[...], wb_ref[1])
    acc += gate(g2_ref, 2) * _dot(yr_ref[...], wb_ref[2])
    o_ref[...] = acc.astype(BF16)


def _merge(y_fox, y_g, y_ret, z, w_branch, b_gate, glu_w, glu_b, *, gate_blk, tm, tn):
    t, width = y_fox.shape
    d = w_branch.shape[2]
    y_spec = pl.BlockSpec((tm, width), lambda i, j: (i, 0))
    nb = d // tn
    g_spec = lambda k: pl.BlockSpec((tm, tn), lambda i, j: (i, gate_blk * LANES // tn + k * nb + j))
    vmem = 3 * 2 * tm * width * 2 + 3 * 2 * tm * tn * 2 + 2 * 3 * width * tn * 2 + 2 * width * width * 2 \
        + tm * width * 2 + 2 * tm * tn * 2 + 6 * tm * tn * 4 + 2 * tm * width * 4
    return pl.pallas_call(
        _merge_kernel,
        grid=(t // tm, nb),
        in_specs=[y_spec, y_spec, y_spec, g_spec(0), g_spec(1), g_spec(2),
                  pl.BlockSpec((3, width, tn), lambda i, j: (0, 0, j)),
                  pl.BlockSpec((3, tn), lambda i, j: (0, j)),
                  pl.BlockSpec((width, width), lambda i, j: (0, 0)),
                  pl.BlockSpec((1, width), lambda i, j: (0, 0))],
        out_specs=pl.BlockSpec((tm, tn), lambda i, j: (i, j)),
        out_shape=jax.ShapeDtypeStruct((t, d), BF16),
        scratch_shapes=[pltpu.VMEM((tm, width), BF16)],
        compiler_params=_cparams(("parallel", "arbitrary"), vmem),
        name="branch_merge",
    )(y_fox, y_g, y_ret, z, z, z, w_branch, b_gate, glu_w, glu_b.reshape(1, width))


def _outproj_kernel(m_ref, w_ref, x_ref, mod_ref, o_ref, *, gate_row):
    y = _dot(m_ref[...], w_ref[...])
    o_ref[...] = x_ref[...] + mod_ref[0, gate_row:gate_row + 1, :] * y


def _outproj(merged, w_out, x2, mod3, *, seq, gate_row, tm, tn):
    t, d = x2.shape
    per_b = seq // tm
    vmem = 2 * tm * d * 2 + 2 * d * tn * 2 + 4 * tm * tn * 4 + 2 * tm * tn * 4
    return pl.pallas_call(
        functools.partial(_outproj_kernel, gate_row=gate_row),
        grid=(t // tm, d // tn),
        in_specs=[pl.BlockSpec((tm, d), lambda i, j: (i, 0)),
                  pl.BlockSpec((d, tn), lambda i, j: (0, j)),
                  pl.BlockSpec((tm, tn), lambda i, j: (i, j)),
                  pl.BlockSpec((1, N_MOD, tn), lambda i, j: (i // per_b, 0, j))],
        out_specs=pl.BlockSpec((tm, tn), lambda i, j: (i, j)),
        out_shape=jax.ShapeDtypeStruct((t, d), F32),
        compiler_params=_cparams(("parallel", "arbitrary"), vmem),
        name="out_proj",
    )(merged, w_out, x2, mod3)


def _tiles(seq, d, f):
    tm = min(1024, seq)
    tf = 512
    fp = -(-f // tf) * tf
    return dict(tm=tm, tm_ffn=min(512, seq), tf=tf, fp=fp, tn=min(1024, d), att_tile=min(512, seq), ret_chunk=min(512, seq), tc=min(256, seq))


def kernel(x, c, ada_w, ada_b, norm_w, final_norm_w, ffn_w1, ffn_w3, ffn_w2, w_in, fox_f_bias, ssm_A_re, ssm_A_im, ssm_log_dt, ssm_B_re, ssm_B_im, ssm_C_re, ssm_C_im, ssm_D, glu_w, glu_b, ret_gn_w, w_branch, b_gate, w_out):
    bsz, seq, d = x.shape
    depth = ada_w.shape[0]
    width = d // 2
    heads = width // HEAD_DIM
    f = ffn_w1.shape[-1]
    assert bsz == SUBLANES, "the S5 kernel maps the batch onto the 8 sublanes of a vreg"
    assert heads <= SUBLANES and width % LANES == 0 and seq % RET_CHUNK == 0
    cfg = _tiles(seq, d, f)
    tm, tf, fp, tn = cfg["tm"], cfg["tf"], cfg["fp"], cfg["tn"]
    t = bsz * seq

    pad_f = fp - f
    w1 = jnp.pad(ffn_w1, ((0, 0), (0, 0), (0, 0), (0, pad_f))).astype(BF16)
    w3 = jnp.pad(ffn_w3, ((0, 0), (0, 0), (0, 0), (0, pad_f))).astype(BF16)
    w2 = jnp.pad(ffn_w2, ((0, 0), (0, 0), (0, pad_f), (0, 0))).astype(BF16)
    f_lo = 3 * width
    f_hi = f_lo + heads
    w_main = jnp.concatenate([w_in[:, :, :f_lo], w_in[:, :, f_hi:]], axis=-1).astype(BF16)
    w_f = jnp.pad(w_in[:, :, f_lo:f_hi], ((0, 0), (0, 0), (0, LANES - heads))).astype(BF16)
    f_bias = jnp.pad(fox_f_bias, ((0, 0), (0, LANES - heads)))
    glu_wb = glu_w.astype(BF16)
    w_br = w_branch.astype(BF16)
    w_o = w_out.astype(BF16)
    wblk = width // LANES
    blk = dict(qa=0, ka=wblk, va=2 * wblk, us=3 * wblk, qr=4 * wblk, kr=5 * wblk, vr=6 * wblk,
               gr=7 * wblk, gl=8 * wblk)
    ret_tables = _ret_tables(seq, heads, cfg["ret_chunk"])
    aug_tables = _fox_aug_tables(heads)

    x2 = x.reshape(t, d)
    for l in range(depth):
        mod3 = _ada(c, ada_w[l], ada_b[l]).reshape(bsz, N_MOD, d)
        x2 = _ffn(x2, mod3, norm_w[l, 0], w1[l, 0], w3[l, 0], w2[l, 0], None,
                  seq=seq, mod_base=0, tm=cfg["tm_ffn"], tf=tf)
        z, fa = _proj(x2, mod3, norm_w[l, 1], w_main[l], w_f[l], seq=seq, mod_base=3, tm=tm, tn=tn)
        z3 = z.reshape(bsz, seq, z.shape[1])
        q_aug, k_aug = _cumf(fa, f_bias[l:l + 1], aug_tables, bsz=bsz, seq=seq, heads=heads)
        y_fox = _fox(z3, q_aug, k_aug, heads=heads,
                     q_blk=blk["qa"], k_blk=blk["ka"], v_blk=blk["va"], tile=cfg["att_tile"])
        ab_re, ab_im, bb_re, bb_im = _ssm_disc(ssm_A_re[l], ssm_A_im[l], ssm_log_dt[l], ssm_B_re[l], ssm_B_im[l])
        wb, wc, ar, ai = _ssm_weights(ab_re, ab_im, bb_re, bb_im, ssm_C_re[l], ssm_C_im[l])
        u_tm = jnp.transpose(z3[:, :, blk["us"] * LANES:(blk["us"] + wblk) * LANES], (1, 0, 2)).reshape(t, width)
        y_tm = _ssm(u_tm, wb, wc, ar, ai, ssm_D[l], seq=seq, tc=cfg["tc"])
        y_g = jnp.transpose(y_tm.reshape(seq, bsz, width), (1, 0, 2)).reshape(t, width)
        y_ret = _ret(z3, ret_tables, ret_gn_w[l], heads=heads,
                     q_blk=blk["qr"], k_blk=blk["kr"], v_blk=blk["vr"], g_blk=blk["gr"],
                     chunk=cfg["ret_chunk"])
        merged = _merge(y_fox.reshape(t, width), y_g, y_ret.reshape(t, width), z, w_br[l], b_gate[l],
                        glu_wb[l], glu_b[l], gate_blk=blk["gl"], tm=tm, tn=min(512, d))
        x2 = _outproj(merged, w_o[l], x2, mod3, seq=seq, gate_row=5, tm=tm, tn=tn)
        x2 = _ffn(x2, mod3, norm_w[l, 2], w1[l, 1], w3[l, 1], w2[l, 1],
                  final_norm_w if l == depth - 1 else None, seq=seq, mod_base=6, tm=cfg["tm_ffn"], tf=tf)
    return x2.reshape(bsz, seq, d)
```

```python
import functools
import math

import jax
import jax.numpy as jnp
from jax import lax
from jax.experimental import pallas as pl
from jax.experimental.pallas import tpu as pltpu

F32 = jnp.float32
BF16 = jnp.bfloat16

EPS = 1e-6
HEAD_DIM = 128
SSM_GROUP = 16
SSM_STATE = 64
SSM_CHUNK = 16
N_MOD = 9
RET_CHUNK = 128
RET_DECAY_BASE = 5.0
ROPE_BASE = 10000.0
LANES = 128
SUBLANES = 8
V7X_VMEM_BUDGET = 56 * 1024 * 1024
NEG_BIG = -1e30
LOG2E = math.log2(math.e)


def _cparams(semantics, vmem_bytes):
    vmem = int(min(max(vmem_bytes, 16 * 1024 * 1024), V7X_VMEM_BUDGET))
    return pltpu.CompilerParams(dimension_semantics=semantics, vmem_limit_bytes=vmem)


def _dot(a, b):
    return jnp.dot(a, b, preferred_element_type=F32)


def _norm_mod(x, nw, sh, sc):
    ms = jnp.mean(x * x, axis=-1, keepdims=True)
    return (x * lax.rsqrt(ms + EPS)) * nw * (1.0 + sc) + sh


def _ada_kernel(c_ref, w_ref, b_ref, o_ref):
    c = c_ref[...]
    ca = (c * jax.nn.sigmoid(c)).astype(BF16)
    o_ref[...] = _dot(ca, w_ref[...].astype(BF16)) + b_ref[...]


def _ada(c, ada_w, ada_b, layer):
    bsz, d = c.shape
    depth, _, n = ada_w.shape
    tn = 1024 if n % 1024 == 0 else n
    return pl.pallas_call(
        _ada_kernel,
        grid=(n // tn,),
        in_specs=[pl.BlockSpec((bsz, d), lambda j: (0, 0)),
                  pl.BlockSpec((None, d, tn), lambda j: (layer, 0, j)),
                  pl.BlockSpec((None, 1, tn), lambda j: (layer, 0, j))],
        out_specs=pl.BlockSpec((bsz, tn), lambda j: (0, j)),
        out_shape=jax.ShapeDtypeStruct((bsz, n), F32),
        compiler_params=_cparams(("arbitrary",), 4 * d * tn * 4),
        name="ada_mod",
    )(c, ada_w, ada_b.reshape(depth, 1, n))


def _ffn_kernel(x_ref, mod_ref, nw_ref, w1_ref, w3_ref, w2_ref, *rest, mod_base, final):
    if final:
        fw_ref, o_ref, h_ref = rest
    else:
        o_ref, h_ref = rest
    j = pl.program_id(1)

    @pl.when(j == 0)
    def _():
        sh = mod_ref[0, mod_base:mod_base + 1, :]
        sc = mod_ref[0, mod_base + 1:mod_base + 2, :]
        h_ref[...] = _norm_mod(x_ref[...], nw_ref[...], sh, sc).astype(BF16)
        o_ref[...] = jnp.zeros_like(o_ref)

    h = h_ref[...]
    a = _dot(h, w1_ref[...])
    b = _dot(h, w3_ref[...])
    g = (a * jax.nn.sigmoid(a) * b).astype(BF16)
    o_ref[...] += _dot(g, w2_ref[...])

    @pl.when(j == pl.num_programs(1) - 1)
    def _():
        gate = mod_ref[0, mod_base + 2:mod_base + 3, :]
        y = x_ref[...] + 0.5 * gate * o_ref[...]
        if final:
            ms = jnp.mean(y * y, axis=-1, keepdims=True)
            y = y * lax.rsqrt(ms + EPS) * fw_ref[...]
        o_ref[...] = y


def _ffn(x2, mod3, norm_w4, w1, w3, w2, final_w, *, layer, which, seq, tm, tf):
    t, d = x2.shape
    fp = w1.shape[3]
    mod_base = 6 * which
    per_b = seq // tm
    final = final_w is not None
    in_specs = [pl.BlockSpec((tm, d), lambda i, j: (i, 0)),
                pl.BlockSpec((1, N_MOD, d), lambda i, j: (i // per_b, 0, 0)),
                pl.BlockSpec((None, None, 1, d), lambda i, j: (layer, 2 * which, 0, 0)),
                pl.BlockSpec((None, None, d, tf), lambda i, j: (layer, which, 0, j)),
                pl.BlockSpec((None, None, d, tf), lambda i, j: (layer, which, 0, j)),
                pl.BlockSpec((None, None, tf, d), lambda i, j: (layer, which, j, 0))]
    args = [x2, mod3, norm_w4, w1, w3, w2]
    if final:
        in_specs.append(pl.BlockSpec((1, d), lambda i, j: (0, 0)))
        args.append(final_w.reshape(1, d))
    vmem = 4 * tm * d * 4 + tm * d * 2 + 6 * d * tf * 2 + 3 * tm * tf * 4 + 2 * tm * d * 4
    return pl.pallas_call(
        functools.partial(_ffn_kernel, mod_base=mod_base, final=final),
        grid=(t // tm, fp // tf),
        in_specs=in_specs,
        out_specs=pl.BlockSpec((tm, d), lambda i, j: (i, 0)),
        out_shape=jax.ShapeDtypeStruct((t, d), F32),
        scratch_shapes=[pltpu.VMEM((tm, d), BF16)],
        compiler_params=_cparams(("parallel", "arbitrary"), vmem),
        name="ffn_final" if final else "ffn",
    )(*args)


def _proj_kernel(x_ref, mod_ref, nw_ref, wa_ref, wb_ref, wf_ref, z_ref, f_ref, h_ref, *, mod_base, n_a):
    j = pl.program_id(1)

    @pl.when(j == 0)
    def _():
        sh = mod_ref[0, mod_base:mod_base + 1, :]
        sc = mod_ref[0, mod_base + 1:mod_base + 2, :]
        h = _norm_mod(x_ref[...], nw_ref[...], sh, sc).astype(BF16)
        h_ref[...] = h
        f_ref[...] = _dot(h, wf_ref[...])

    @pl.when(j < n_a)
    def _():
        z_ref[...] = _dot(h_ref[...], wa_ref[...]).astype(BF16)

    @pl.when(j >= n_a)
    def _():
        z_ref[...] = _dot(h_ref[...], wb_ref[...]).astype(BF16)


def _proj(x2, mod3, norm_w4, w_a, w_b, w_f, *, layer, seq, mod_base, tm, tn):
    t, d = x2.shape
    n_a = w_a.shape[2] // tn
    nz = w_a.shape[2] + w_b.shape[2]
    per_b = seq // tm
    vmem = 2 * tm * d * 4 + tm * d * 2 + 4 * d * tn * 2 + 2 * tm * tn * 2 + 2 * (tm * LANES * 4 + d * LANES * 2) \
        + 2 * tm * tn * 4 + tm * d * 4
    return pl.pallas_call(
        functools.partial(_proj_kernel, mod_base=mod_base, n_a=n_a),
        grid=(t // tm, nz // tn),
        in_specs=[pl.BlockSpec((tm, d), lambda i, j: (i, 0)),
                  pl.BlockSpec((1, N_MOD, d), lambda i, j: (i // per_b, 0, 0)),
                  pl.BlockSpec((None, None, 1, d), lambda i, j: (layer, 1, 0, 0)),
                  pl.BlockSpec((None, d, tn), lambda i, j: (layer, 0, jnp.minimum(j, n_a - 1))),
                  pl.BlockSpec((None, d, tn), lambda i, j: (layer, 0, jnp.maximum(j - n_a, 0))),
                  pl.BlockSpec((None, d, LANES), lambda i, j: (layer, 0, 0))],
        out_specs=[pl.BlockSpec((tm, tn), lambda i, j: (i, j)),
                   pl.BlockSpec((tm, LANES), lambda i, j: (i, 0))],
        out_shape=[jax.ShapeDtypeStruct((t, nz), BF16),
                   jax.ShapeDtypeStruct((t, LANES), F32)],
        scratch_shapes=[pltpu.VMEM((tm, d), BF16)],
        compiler_params=_cparams(("parallel", "arbitrary"), vmem),
        name="in_proj",
    )(x2, mod3, norm_w4, w_a, w_b, w_f)


def _split3(x):
    hi = x.astype(BF16)
    r1 = x - hi.astype(F32)
    mid = r1.astype(BF16)
    lo = (r1 - mid.astype(F32)).astype(BF16)
    return hi, mid, lo


def _cumf_kernel(f_ref, fb_ref, eq_ref, ek_ref, cq_ref, ck_ref, qa_ref, ka_ref, *, chunk):
    s_len = f_ref.shape[0]
    row = lax.broadcasted_iota(jnp.int32, (chunk, chunk), 0)
    col = lax.broadcasted_iota(jnp.int32, (chunk, chunk), 1)
    tri = jnp.where(row >= col, 1.0, 0.0).astype(BF16)
    carry = jnp.zeros((1, LANES), F32)
    for ci in range(s_len // chunk):
        rows = slice(ci * chunk, (ci + 1) * chunk)
        x = f_ref[rows, :] + fb_ref[...]
        ls = jnp.minimum(x, 0.0) - jnp.log(1.0 + jnp.exp(-jnp.abs(x)))
        hi, mid, lo = _split3(ls)
        cs = _dot(tri, hi) + _dot(tri, mid) + _dot(tri, lo) + carry
        carry = cs[chunk - 1:chunk, :]
        pieces = jnp.concatenate(_split3(cs * LOG2E), axis=1)
        qa_ref[0, rows, :] = (_dot(pieces, eq_ref[...]) + cq_ref[...]).astype(BF16)
        ka_ref[0, rows, :] = (_dot(pieces, ek_ref[...]) + ck_ref[...]).astype(BF16)


def _fox_aug_tables(heads):
    n = heads * HEAD_DIM
    eq = jnp.zeros((3 * LANES, n), F32)
    ek = jnp.zeros((3 * LANES, n), F32)
    cq = jnp.zeros((1, n), F32)
    ck = jnp.zeros((1, n), F32)
    h = jnp.arange(heads)
    for piece in range(3):
        eq = eq.at[piece * LANES + h, h * HEAD_DIM + piece].set(1.0)
        ek = ek.at[piece * LANES + h, h * HEAD_DIM + 3 + piece].set(-1.0)
        cq = cq.at[0, h * HEAD_DIM + 3 + piece].set(1.0)
        ck = ck.at[0, h * HEAD_DIM + piece].set(1.0)
    return eq.astype(BF16), ek.astype(BF16), cq, ck


def _cumf(fa, f_bias_pad, aug_tables, *, bsz, seq, heads):
    chunk = 256 if seq % 256 == 0 else seq
    n = heads * HEAD_DIM
    eq, ek, cq, ck = aug_tables
    e_spec = pl.BlockSpec((3 * LANES, n), lambda b: (0, 0))
    c_spec = pl.BlockSpec((1, n), lambda b: (0, 0))
    o_spec = pl.BlockSpec((1, seq, n), lambda b: (b, 0, 0))
    return pl.pallas_call(
        functools.partial(_cumf_kernel, chunk=chunk),
        grid=(bsz,),
        in_specs=[pl.BlockSpec((seq, LANES), lambda b: (b, 0)),
                  pl.BlockSpec((1, LANES), lambda b: (0, 0)), e_spec, e_spec, c_spec, c_spec],
        out_specs=[o_spec, o_spec],
        out_shape=[jax.ShapeDtypeStruct((bsz, seq, n), BF16)] * 2,
        compiler_params=_cparams(("parallel",), 8 * seq * n * 2 + 8 * seq * LANES * 4),
        name="fox_cum_logf",
    )(fa, f_bias_pad, eq, ek, cq, ck)


def _fox_kernel(q_ref, k_ref, v_ref, qa_ref, ka_ref, o_ref, *, tile, scale):
    seq = q_ref.shape[1]
    qs = (q_ref[0].astype(F32) * (scale * LOG2E)).astype(BF16)
    qp = jnp.concatenate([qs, qa_ref[0]], axis=1)
    kp = jnp.concatenate([k_ref[0], ka_ref[0]], axis=1)
    r = lax.broadcasted_iota(jnp.int32, (tile, tile), 0)
    c = lax.broadcasted_iota(jnp.int32, (tile, tile), 1)
    causal = c <= r
    for i in range(seq // tile):
        qi = qp[i * tile:(i + 1) * tile]
        m = l = acc = None
        for j in range(i + 1):
            keys = slice(j * tile, (j + 1) * tile)
            s = lax.dot_general(qi, kp[keys], (((1,), (1,)), ((), ())), preferred_element_type=F32)
            if j == i:
                s = jnp.where(causal, s, NEG_BIG)
            m_tile = jnp.max(s, axis=-1, keepdims=True)
            if j == 0:
                m = m_tile
                p = jnp.exp2(s - m)
                l = jnp.sum(p, axis=-1, keepdims=True)
                acc = _dot(p.astype(BF16), v_ref[0, keys, :])
            else:
                m_new = jnp.maximum(m, m_tile)
                alpha = jnp.exp2(m - m_new)
                p = jnp.exp2(s - m_new)
                l = alpha * l + jnp.sum(p, axis=-1, keepdims=True)
                acc = alpha * acc + _dot(p.astype(BF16), v_ref[0, keys, :])
                m = m_new
        o_ref[0, i * tile:(i + 1) * tile, :] = (acc / l).astype(BF16)


def _fox(z3, q_aug, k_aug, *, heads, q_blk, k_blk, v_blk, tile):
    bsz, seq, _ = z3.shape
    width = heads * HEAD_DIM
    head_spec = lambda blk: pl.BlockSpec((1, seq, HEAD_DIM), lambda b, h: (b, 0, blk + h))
    return pl.pallas_call(
        functools.partial(_fox_kernel, tile=tile, scale=HEAD_DIM ** -0.5),
        grid=(bsz, heads),
        in_specs=[head_spec(q_blk), head_spec(k_blk), head_spec(v_blk), head_spec(0), head_spec(0)],
        out_specs=head_spec(0),
        out_shape=jax.ShapeDtypeStruct((bsz, seq, width), BF16),
        compiler_params=_cparams(("parallel", "parallel"), 40 << 20),
        name="fox_attention",
    )(z3, z3, z3, q_aug, k_aug)


def _zoh(a_re, a_im, log_dt):
    dt = jnp.exp(log_dt)
    mag = jnp.exp(a_re * dt)
    ab_re = mag * jnp.cos(a_im * dt)
    ab_im = mag * jnp.sin(a_im * dt)
    den = a_re * a_re + a_im * a_im
    num_re = ab_re - 1.0
    coef_re = (num_re * a_re + ab_im * a_im) / den
    coef_im = (ab_im * a_re - num_re * a_im) / den
    return ab_re, ab_im, coef_re, coef_im


def _ssm_pow_kernel(are_ref, aim_ref, ldt_ref, bre_ref, bim_ref, are2_ref, aim2_ref, ldt2_ref, cre_ref, cim_ref,
                    rre_ref, rim_ref, alr_ref, ali_ref, ere_ref, eim_ref):
    ab_re, ab_im, coef_re, coef_im = _zoh(are_ref[...], aim_ref[...], ldt_ref[...])
    b_re = bre_ref[...]
    b_im = bim_ref[...]
    r_re = coef_re * b_re - coef_im * b_im
    r_im = coef_re * b_im + coef_im * b_re
    pw_re = jnp.ones_like(ab_re)
    pw_im = jnp.zeros_like(ab_re)
    for tau in range(SSM_CHUNK):
        rre_ref[tau] = pw_re * r_re - pw_im * r_im
        rim_ref[tau] = pw_re * r_im + pw_im * r_re
        pw_re, pw_im = pw_re * ab_re - pw_im * ab_im, pw_re * ab_im + pw_im * ab_re
    alr_ref[...] = pw_re
    ali_ref[...] = pw_im
    ab_re, ab_im, _, _ = _zoh(are2_ref[...], aim2_ref[...], ldt2_ref[...])
    c_re = cre_ref[...]
    c_im = cim_ref[...]
    pw_re, pw_im = ab_re, ab_im
    for i in range(SSM_CHUNK):
        ere_ref[i] = c_re * pw_re - c_im * pw_im
        eim_ref[i] = c_re * pw_im + c_im * pw_re
        pw_re, pw_im = pw_re * ab_re - pw_im * ab_im, pw_re * ab_im + pw_im * ab_re


def _ssm_kgen_kernel(c_ref, r_ref, k_ref):
    for gi in range(c_ref.shape[0]):
        k_ref[gi] = _dot(c_ref[gi], r_ref[gi])


def _ssm_params(a_re, a_im, log_dt, b_re, b_im, c_re, c_im, d_skip):
    g, p, n = b_re.shape
    ell = SSM_CHUNK
    e = p * n
    rep = lambda a: jnp.repeat(a, n, axis=-1)
    til = lambda a: jnp.tile(a, (1, n))
    ldt = jnp.broadcast_to(log_dt[:, None], (g, e))
    full = pl.BlockSpec((g, e), lambda: (0, 0))
    stack = pl.BlockSpec((ell, g, e), lambda: (0, 0, 0))
    r_re, r_im, al_re, al_im, e_re, e_im = pl.pallas_call(
        _ssm_pow_kernel,
        in_specs=[full] * 10,
        out_specs=[stack, stack, full, full, stack, stack],
        out_shape=[jax.ShapeDtypeStruct((ell, g, e), F32)] * 2 + [jax.ShapeDtypeStruct((g, e), F32)] * 2
        + [jax.ShapeDtypeStruct((ell, g, e), F32)] * 2,
        compiler_params=_cparams((), 12 * ell * g * e * 4),
        name="ssm_powers",
    )(rep(a_re), rep(a_im), ldt, b_re.reshape(g, e), b_im.reshape(g, e),
      til(a_re), til(a_im), ldt, c_re.reshape(g, e), c_im.reshape(g, e))

    r5 = jnp.stack([r_re, r_im], axis=0).reshape(2, ell, g, p, n)
    r_mat = r5.transpose(2, 0, 3, 1, 4).reshape(g, 2 * p, ell * n)
    c_mat = jnp.concatenate([c_re, -c_im], axis=-1)
    gb = 8 if g % 8 == 0 else g
    k_mat = pl.pallas_call(
        _ssm_kgen_kernel,
        grid=(g // gb,),
        in_specs=[pl.BlockSpec((gb, n, 2 * p), lambda i: (i, 0, 0)),
                  pl.BlockSpec((gb, 2 * p, ell * n), lambda i: (i, 0, 0))],
        out_specs=pl.BlockSpec((gb, n, ell * n), lambda i: (i, 0, 0)),
        out_shape=jax.ShapeDtypeStruct((g, n, ell * n), F32),
        name="ssm_kernel_gen",
    )(c_mat.astype(BF16), r_mat.astype(BF16))
    k4 = k_mat.reshape(g, n, ell, n).transpose(0, 2, 3, 1)
    lag = jnp.arange(ell)[None, :] - jnp.arange(ell)[:, None]
    toe = jnp.where((lag >= 0)[None, :, :, None, None], k4[:, jnp.maximum(lag, 0)], 0.0)
    t_mat = toe.transpose(0, 1, 3, 2, 4).reshape(g, ell * n, ell * n)
    eye = jnp.eye(2, dtype=F32)
    ws = r5[:, ::-1].transpose(2, 1, 4, 0, 3).reshape(g // 2, 2, ell * n, 2, p)
    ws_pair = jnp.einsum('qgkrp,gh->qgkrhp', ws, eye).reshape(g // 2, 2 * ell * n, 4 * p)
    e5 = jnp.stack([e_re, -e_im], axis=0).reshape(2, ell, g, n, p)
    wy = e5.transpose(2, 0, 4, 1, 3).reshape(g // 2, 2, 2, p, ell * n)
    wy_pair = jnp.einsum('qgrpx,gh->qrgphx', wy, eye).reshape(g // 2, 4 * p, 2 * ell * n)
    pair_lanes = lambda a: jnp.broadcast_to(a[:, ::n].reshape(g // 2, 1, 2 * p), (g // 2, SUBLANES, 2 * p))
    d_e = jnp.broadcast_to(d_skip.reshape(g // 2, 2, 1, n), (g // 2, 2, ell, n)).reshape(g // 2, 1, 2 * ell * n)
    return (t_mat.astype(BF16), ws_pair.astype(BF16), wy_pair.astype(BF16),
            pair_lanes(al_re), pair_lanes(al_im), d_e)


def _gelu_tanh(x):
    return 0.5 * x * (1.0 + jnp.tanh(math.sqrt(2.0 / math.pi) * (x + 0.044715 * (x * x * x))))


def _ssm_kernel(u_ref, t_ref, ws_ref, wy_ref, ar_ref, ai_ref, d_ref, o_ref, sv_ref, *, n_chunks):
    cw = u_ref.shape[2]
    half = sv_ref.shape[1] // 2
    u0 = u_ref[0]
    u1 = u_ref[1]
    up = jnp.concatenate([u0, u1], axis=1)
    sv_ref[...] = _dot(up, ws_ref[0])
    ar = ar_ref[0]
    ai = ai_ref[0]

    def step(c, carry):
        sr, si = carry
        r = pl.multiple_of(c * SUBLANES, SUBLANES)
        vr = sv_ref[pl.ds(r, SUBLANES), 0:half]
        vi = sv_ref[pl.ds(r, SUBLANES), half:2 * half]
        sv_ref[pl.ds(r, SUBLANES), 0:half] = sr
        sv_ref[pl.ds(r, SUBLANES), half:2 * half] = si
        return ar * sr - ai * si + vr, ar * si + ai * sr + vi

    zero = jnp.zeros((SUBLANES, half), F32)
    lax.fori_loop(0, n_chunks, step, (zero, zero), unroll=8)
    y = _dot(sv_ref[...].astype(BF16), wy_ref[0])
    y = y + jnp.concatenate([_dot(u0, t_ref[0]), _dot(u1, t_ref[1])], axis=1)
    y = y + d_ref[0] * up.astype(F32)
    act = _gelu_tanh(y).astype(BF16)
    o_ref[0] = act[:, 0:cw]
    o_ref[1] = act[:, cw:2 * cw]


def _ssm(u_c, params, *, n_chunks):
    g, rows, cw = u_c.shape
    t_mat, ws_pair, wy_pair, al_re, al_im, d_e = params
    sw = ws_pair.shape[2]
    pair = lambda r, c: pl.BlockSpec((1, r, c), lambda q: (q, 0, 0))
    return pl.pallas_call(
        functools.partial(_ssm_kernel, n_chunks=n_chunks),
        grid=(g // 2,),
        in_specs=[pl.BlockSpec((2, rows, cw), lambda q: (q, 0, 0)),
                  pl.BlockSpec((2, cw, cw), lambda q: (q, 0, 0)),
                  pair(2 * cw, sw), pair(sw, 2 * cw), pair(SUBLANES, sw // 2), pair(SUBLANES, sw // 2),
                  pair(1, 2 * cw)],
        out_specs=pl.BlockSpec((2, rows, cw), lambda q: (q, 0, 0)),
        out_shape=jax.ShapeDtypeStruct((g, rows, cw), BF16),
        scratch_shapes=[pltpu.VMEM((rows, sw), F32)],
        compiler_params=_cparams(("parallel",), 8 * rows * cw * 2 + rows * sw * 4 + 6 * rows * 2 * cw * 4),
        name="s5_ssm",
    )(u_c, t_mat, ws_pair, wy_pair, al_re, al_im, d_e)


def _ret_kernel(q_ref, k_ref, v_ref, g_ref, cos_ref, sin_ref, intra_ref, qd_ref, kd_ref, cd_ref, gn_ref,
                o_ref, *, chunk, scale):
    seq = q_ref.shape[1]
    half = HEAD_DIM // 2
    cs = cos_ref[...]
    sn = sin_ref[...]

    def rot(t):
        return t * cs + pltpu.roll(t, half, 1) * sn

    qr = rot(q_ref[0].astype(F32))
    kr = rot(k_ref[0].astype(F32)) * scale
    qb = qr.astype(BF16)
    kb = kr.astype(BF16)
    intra = intra_ref[0]
    qd = qd_ref[0]
    kd = kd_ref[0]
    cd = cd_ref[0, 0:1, :]
    gn = gn_ref[...]
    n_chunks = seq // chunk
    state = None
    for ci in range(n_chunks):
        rows = slice(ci * chunk, (ci + 1) * chunk)
        vc = v_ref[0, rows, :]
        scores = lax.dot_general(qb[rows], kb[rows], (((1,), (1,)), ((), ())),
                                 preferred_element_type=F32) * intra
        o = _dot(scores.astype(BF16), vc)
        if state is not None:
            o = o + _dot((qr[rows] * qd).astype(BF16), state.astype(BF16))
        if ci + 1 < n_chunks:
            kv = lax.dot_general((kr[rows] * kd).astype(BF16), vc, (((0,), (0,)), ((), ())),
                                 preferred_element_type=F32)
            state = kv if state is None else cd * state + kv
        mu = jnp.mean(o, axis=-1, keepdims=True)
        oc = o - mu
        var = jnp.mean(oc * oc, axis=-1, keepdims=True)
        on = oc * lax.rsqrt(var + EPS)
        gr = g_ref[0, rows, :].astype(F32)
        o_ref[0, rows, :] = ((on * gn) * (gr * jax.nn.sigmoid(gr))).astype(BF16)


def _ret_tables(seq, heads, chunk):
    half = HEAD_DIM // 2
    pos = jnp.arange(seq, dtype=F32)
    inv_freq = 1.0 / (ROPE_BASE ** jnp.linspace(0.0, 1.0, half, dtype=F32))
    ang = pos[:, None] * inv_freq[None, :]
    cos = jnp.cos(ang)
    sin = jnp.sin(ang)
    cos2 = jnp.concatenate([cos, cos], axis=-1)
    sin2 = jnp.concatenate([-sin, sin], axis=-1)
    log_gamma = jnp.log(1.0 - jnp.exp2(-RET_DECAY_BASE - jnp.arange(heads, dtype=F32)))
    idx = jnp.arange(chunk, dtype=F32)
    rel = idx[:, None] - idx[None, :]
    intra = jnp.where(rel[None] >= 0, jnp.exp(jnp.maximum(rel, 0.0)[None] * log_gamma[:, None, None]), 0.0)
    q_decay = jnp.exp((idx + 1.0)[None, :] * log_gamma[:, None])
    k_decay = jnp.exp((chunk - 1.0 - idx)[None, :] * log_gamma[:, None])
    chunk_decay = jnp.exp(chunk * log_gamma)
    qd = jnp.broadcast_to(q_decay[:, :, None], (heads, chunk, HEAD_DIM))
    kd = jnp.broadcast_to(k_decay[:, :, None], (heads, chunk, HEAD_DIM))
    cdec = jnp.broadcast_to(chunk_decay[:, None, None], (heads, SUBLANES, HEAD_DIM))
    return cos2, sin2, intra, qd, kd, cdec


def _ret(z3, tables, gn_w, *, heads, q_blk, k_blk, v_blk, g_blk, chunk):
    bsz, seq, _ = z3.shape
    width = heads * HEAD_DIM
    cos2, sin2, intra, qd, kd, cdec = tables
    head_spec = lambda blk: pl.BlockSpec((1, seq, HEAD_DIM), lambda h, b: (b, 0, blk + h))
    tab_spec = pl.BlockSpec((seq, HEAD_DIM), lambda h, b: (0, 0))
    per_head = lambda rows, cols: pl.BlockSpec((1, rows, cols), lambda h, b: (h, 0, 0))
    return pl.pallas_call(
        functools.partial(_ret_kernel, chunk=chunk, scale=HEAD_DIM ** -0.5),
        grid=(heads, bsz),
        in_specs=[head_spec(q_blk), head_spec(k_blk), head_spec(v_blk), head_spec(g_blk),
                  tab_spec, tab_spec, per_head(chunk, chunk), per_head(chunk, HEAD_DIM),
                  per_head(chunk, HEAD_DIM), per_head(SUBLANES, HEAD_DIM),
                  pl.BlockSpec((1, HEAD_DIM), lambda h, b: (0, h))],
        out_specs=pl.BlockSpec((1, seq, HEAD_DIM), lambda h, b: (b, 0, h)),
        out_shape=jax.ShapeDtypeStruct((bsz, seq, width), BF16),
        compiler_params=_cparams(("parallel", "parallel"), 40 << 20),
        name="retention",
    )(z3, z3, z3, z3, cos2, sin2, intra, qd, kd, cdec, gn_w.reshape(1, width))


def _merge_kernel(yf_ref, yg_ref, yr_ref, g0_ref, g1_ref, g2_ref, wb_ref, bg_ref, gw_ref, gb_ref,
                  o_ref, ys_ref):
    j = pl.program_id(1)

    @pl.when(j == 0)
    def _():
        yg = yg_ref[...]
        t = _dot(yg, gw_ref[...]) + gb_ref[...]
        ys_ref[...] = (yg.astype(F32) * jax.nn.sigmoid(t)).astype(BF16)

    def gate(g_ref, k):
        return jax.nn.sigmoid(g_ref[...].astype(F32) + bg_ref[k:k + 1, :])

    acc = gate(g0_ref, 0) * _dot(yf_ref[...], wb_ref[0])
    acc += gate(g1_ref, 1) * _dot(ys_ref[...], wb_ref[1])
    acc += gate(g2_ref, 2) * _dot(yr_ref[...], wb_ref[2])
    o_ref[...] = acc.astype(BF16)


def _merge(y_fox, y_g, y_ret, z, w_branch, b_gate, glu_w, glu_b, *, layer, gate_blk, tm, tn):
    t, width = y_fox.shape
    depth, _, _, d = w_branch.shape
    y_spec = pl.BlockSpec((tm, width), lambda i, j: (i, 0))
    nb = d // tn
    g_spec = lambda k: pl.BlockSpec((tm, tn), lambda i, j: (i, gate_blk * LANES // tn + k * nb + j))
    vmem = 3 * 2 * tm * width * 2 + 3 * 2 * tm * tn * 2 + 2 * 3 * width * tn * 2 + 2 * width * width * 2 \
        + tm * width * 2 + 2 * tm * tn * 2 + 6 * tm * tn * 4 + 2 * tm * width * 4
    return pl.pallas_call(
        _merge_kernel,
        grid=(t // tm, nb),
        in_specs=[y_spec, y_spec, y_spec, g_spec(0), g_spec(1), g_spec(2),
                  pl.BlockSpec((None, 3, width, tn), lambda i, j: (layer, 0, 0, j)),
                  pl.BlockSpec((None, 3, tn), lambda i, j: (layer, 0, j)),
                  pl.BlockSpec((None, width, width), lambda i, j: (layer, 0, 0)),
                  pl.BlockSpec((None, 1, width), lambda i, j: (layer, 0, 0))],
        out_specs=pl.BlockSpec((tm, tn), lambda i, j: (i, j)),
        out_shape=jax.ShapeDtypeStruct((t, d), BF16),
        scratch_shapes=[pltpu.VMEM((tm, width), BF16)],
        compiler_params=_cparams(("parallel", "arbitrary"), vmem),
        name="branch_merge",
    )(y_fox, y_g, y_ret, z, z, z, w_branch, b_gate, glu_w, glu_b.reshape(depth, 1, width))


def _outproj_kernel(m_ref, w_ref, x_ref, mod_ref, o_ref, *, gate_row):
    y = _dot(m_ref[...], w_ref[...])
    o_ref[...] = x_ref[...] + mod_ref[0, gate_row:gate_row + 1, :] * y


def _outproj(merged, w_out, x2, mod3, *, layer, seq, gate_row, tm, tn):
    t, d = x2.shape
    per_b = seq // tm
    vmem = 2 * tm * d * 2 + 2 * d * tn * 2 + 4 * tm * tn * 4 + 2 * tm * tn * 4
    return pl.pallas_call(
        functools.partial(_outproj_kernel, gate_row=gate_row),
        grid=(t // tm, d // tn),
        in_specs=[pl.BlockSpec((tm, d), lambda i, j: (i, 0)),
                  pl.BlockSpec((None, d, tn), lambda i, j: (layer, 0, j)),
                  pl.BlockSpec((tm, tn), lambda i, j: (i, j)),
                  pl.BlockSpec((1, N_MOD, tn), lambda i, j: (i // per_b, 0, j))],
        out_specs=pl.BlockSpec((tm, tn), lambda i, j: (i, j)),
        out_shape=jax.ShapeDtypeStruct((t, d), F32),
        compiler_params=_cparams(("parallel", "arbitrary"), vmem),
        name="out_proj",
    )(merged, w_out, x2, mod3)


def _tiles(seq, d, f):
    tm = min(1024, seq)
    tf = 512
    fp = -(-f // tf) * tf
    width = d // 2
    tn_proj = next(c for c in (1024, 512, 256, 128) if (3 * width) % c == 0 and (5 * width + 3 * d) % c == 0)
    return dict(tm=tm, tm_ffn=min(512, seq), tf=tf, fp=fp, tn=min(1024, d), tn_proj=tn_proj,
                att_tile=min(512, seq), ret_chunk=min(512, seq))


def kernel(x, c, ada_w, ada_b, norm_w, final_norm_w, ffn_w1, ffn_w3, ffn_w2, w_in, fox_f_bias, ssm_A_re, ssm_A_im, ssm_log_dt, ssm_B_re, ssm_B_im, ssm_C_re, ssm_C_im, ssm_D, glu_w, glu_b, ret_gn_w, w_branch, b_gate, w_out):
    bsz, seq, d = x.shape
    depth = ada_w.shape[0]
    width = d // 2
    heads = width // HEAD_DIM
    groups = width // SSM_GROUP
    f = ffn_w1.shape[-1]
    assert bsz == SUBLANES, "the S5 kernel maps the batch onto the 8 sublanes of a vreg"
    assert heads <= SUBLANES and width % LANES == 0 and seq % RET_CHUNK == 0 and groups % 2 == 0
    cfg = _tiles(seq, d, f)
    tm, tf, fp, tn = cfg["tm"], cfg["tf"], cfg["fp"], cfg["tn"]
    t = bsz * seq

    pad_f = fp - f
    w1 = jnp.pad(ffn_w1, ((0, 0), (0, 0), (0, 0), (0, pad_f))).astype(BF16)
    w3 = jnp.pad(ffn_w3, ((0, 0), (0, 0), (0, 0), (0, pad_f))).astype(BF16)
    w2 = jnp.pad(ffn_w2, ((0, 0), (0, 0), (0, pad_f), (0, 0))).astype(BF16)
    f_lo = 3 * width
    f_hi = f_lo + heads
    w_a = w_in[:, :, :f_lo].astype(BF16)
    w_b = w_in[:, :, f_hi:].astype(BF16)
    w_f = jnp.pad(w_in[:, :, f_lo:f_hi], ((0, 0), (0, 0), (0, LANES - heads))).astype(BF16)
    f_bias = jnp.pad(fox_f_bias, ((0, 0), (0, LANES - heads)))
    glu_wb = glu_w.astype(BF16)
    w_br = w_branch.astype(BF16)
    w_o = w_out.astype(BF16)
    wblk = width // LANES
    blk = dict(qa=0, ka=wblk, va=2 * wblk, us=3 * wblk, qr=4 * wblk, kr=5 * wblk, vr=6 * wblk,
               gr=7 * wblk, gl=8 * wblk)
    ret_tables = _ret_tables(seq, heads, cfg["ret_chunk"])
    aug_tables = _fox_aug_tables(heads)

    norm_w4 = norm_w.reshape(depth, 3, 1, d)
    x2 = x.reshape(t, d)
    for l in range(depth):
        mod3 = _ada(c, ada_w, ada_b, l).reshape(bsz, N_MOD, d)
        x2 = _ffn(x2, mod3, norm_w4, w1, w3, w2, None, layer=l, which=0, seq=seq, tm=cfg["tm_ffn"], tf=tf)
        z, fa = _proj(x2, mod3, norm_w4, w_a, w_b, w_f, layer=l, seq=seq, mod_base=3, tm=tm, tn=cfg["tn_proj"])
        z3 = z.reshape(bsz, seq, z.shape[1])
        q_aug, k_aug = _cumf(fa, f_bias[l:l + 1], aug_tables, bsz=bsz, seq=seq, heads=heads)
        y_fox = _fox(z3, q_aug, k_aug, heads=heads,
                     q_blk=blk["qa"], k_blk=blk["ka"], v_blk=blk["va"], tile=cfg["att_tile"])
        ssm_params = _ssm_params(ssm_A_re[l], ssm_A_im[l], ssm_log_dt[l], ssm_B_re[l], ssm_B_im[l],
                                 ssm_C_re[l], ssm_C_im[l], ssm_D[l])
        n_chunks = seq // SSM_CHUNK
        u5 = z3[:, :, blk["us"] * LANES:(blk["us"] + wblk) * LANES].reshape(bsz, n_chunks, SSM_CHUNK, groups, SSM_GROUP)
        u_c = jnp.transpose(u5, (3, 1, 0, 2, 4)).reshape(groups, n_chunks * bsz, SSM_CHUNK * SSM_GROUP)
        y_c = _ssm(u_c, ssm_params, n_chunks=n_chunks)
        y5 = y_c.reshape(groups, n_chunks, bsz, SSM_CHUNK, SSM_GROUP)
        y_g = jnp.transpose(y5, (2, 1, 3, 0, 4)).reshape(t, width)
        y_ret = _ret(z3, ret_tables, ret_gn_w[l], heads=heads,
                     q_blk=blk["qr"], k_blk=blk["kr"], v_blk=blk["vr"], g_blk=blk["gr"],
                     chunk=cfg["ret_chunk"])
        merged = _merge(y_fox.reshape(t, width), y_g, y_ret.reshape(t, width), z, w_br, b_gate,
                        glu_wb, glu_b, layer=l, gate_blk=blk["gl"], tm=tm, tn=min(512, d))
        x2 = _outproj(merged, w_o, x2, mod3, layer=l, seq=seq, gate_row=5, tm=tm, tn=tn)
        x2 = _ffn(x2, mod3, norm_w4, w1, w3, w2, final_norm_w if l == depth - 1 else None,
                  layer=l, which=1, seq=seq, tm=cfg["tm_ffn"], tf=tf)
    return x2.reshape(bsz, seq, d)
```

```python
import functools
import math

import jax
import jax.numpy as jnp
from jax import lax
from jax.experimental import pallas as pl
from jax.experimental.pallas import tpu as pltpu

F32 = jnp.float32
BF16 = jnp.bfloat16

EPS = 1e-6
HEAD_DIM = 128
SSM_GROUP = 16
SSM_STATE = 64
N_MOD = 9
RET_CHUNK = 128
RET_DECAY_BASE = 5.0
ROPE_BASE = 10000.0
LANES = 128
SUBLANES = 8
V7X_VMEM_BUDGET = 56 * 1024 * 1024
NEG_BIG = -1e30
LOG2E = math.log2(math.e)


def _cparams(semantics, vmem_bytes):
    vmem = int(min(max(vmem_bytes, 16 * 1024 * 1024), V7X_VMEM_BUDGET))
    return pltpu.CompilerParams(dimension_semantics=semantics, vmem_limit_bytes=vmem)


def _dot(a, b):
    return jnp.dot(a, b, preferred_element_type=F32)


def _norm_mod(x, nw, sh, sc):
    ms = jnp.mean(x * x, axis=-1, keepdims=True)
    return (x * lax.rsqrt(ms + EPS)) * nw * (1.0 + sc) + sh


def _ada_kernel(c_ref, w_ref, b_ref, o_ref):
    c = c_ref[...]
    ca = (c * jax.nn.sigmoid(c)).astype(BF16)
    o_ref[...] = _dot(ca, w_ref[...].astype(BF16)) + b_ref[...]


def _ada(c, ada_w, ada_b, layer):
    bsz, d = c.shape
    depth, _, n = ada_w.shape
    tn = 1024 if n % 1024 == 0 else n
    return pl.pallas_call(
        _ada_kernel,
        grid=(n // tn,),
        in_specs=[pl.BlockSpec((bsz, d), lambda j: (0, 0)),
                  pl.BlockSpec((None, d, tn), lambda j: (layer, 0, j)),
                  pl.BlockSpec((None, 1, tn), lambda j: (layer, 0, j))],
        out_specs=pl.BlockSpec((bsz, tn), lambda j: (0, j)),
        out_shape=jax.ShapeDtypeStruct((bsz, n), F32),
        compiler_params=_cparams(("arbitrary",), 4 * d * tn * 4),
        name="ada_mod",
    )(c, ada_w, ada_b.reshape(depth, 1, n))


def _ffn_kernel(x_ref, mod_ref, nw_ref, w1_ref, w3_ref, w2_ref, *rest, mod_base, final):
    if final:
        fw_ref, o_ref, h_ref = rest
    else:
        o_ref, h_ref = rest
    j = pl.program_id(1)

    @pl.when(j == 0)
    def _():
        sh = mod_ref[0, mod_base:mod_base + 1, :]
        sc = mod_ref[0, mod_base + 1:mod_base + 2, :]
        h_ref[...] = _norm_mod(x_ref[...], nw_ref[...], sh, sc).astype(BF16)
        o_ref[...] = jnp.zeros_like(o_ref)

    h = h_ref[...]
    a = _dot(h, w1_ref[...])
    b = _dot(h, w3_ref[...])
    g = (a * jax.nn.sigmoid(a) * b).astype(BF16)
    o_ref[...] += _dot(g, w2_ref[...])

    @pl.when(j == pl.num_programs(1) - 1)
    def _():
        gate = mod_ref[0, mod_base + 2:mod_base + 3, :]
        y = x_ref[...] + 0.5 * gate * o_ref[...]
        if final:
            ms = jnp.mean(y * y, axis=-1, keepdims=True)
            y = y * lax.rsqrt(ms + EPS) * fw_ref[...]
        o_ref[...] = y


def _ffn(x2, mod3, norm_w4, w1, w3, w2, final_w, *, layer, which, seq, tm, tf):
    t, d = x2.shape
    fp = w1.shape[3]
    mod_base = 6 * which
    per_b = seq // tm
    final = final_w is not None
    in_specs = [pl.BlockSpec((tm, d), lambda i, j: (i, 0)),
                pl.BlockSpec((1, N_MOD, d), lambda i, j: (i // per_b, 0, 0)),
                pl.BlockSpec((None, None, 1, d), lambda i, j: (layer, 2 * which, 0, 0)),
                pl.BlockSpec((None, None, d, tf), lambda i, j: (layer, which, 0, j)),
                pl.BlockSpec((None, None, d, tf), lambda i, j: (layer, which, 0, j)),
                pl.BlockSpec((None, None, tf, d), lambda i, j: (layer, which, j, 0))]
    args = [x2, mod3, norm_w4, w1, w3, w2]
    if final:
        in_specs.append(pl.BlockSpec((1, d), lambda i, j: (0, 0)))
        args.append(final_w.reshape(1, d))
    vmem = 4 * tm * d * 4 + tm * d * 2 + 6 * d * tf * 2 + 3 * tm * tf * 4 + 2 * tm * d * 4
    return pl.pallas_call(
        functools.partial(_ffn_kernel, mod_base=mod_base, final=final),
        grid=(t // tm, fp // tf),
        in_specs=in_specs,
        out_specs=pl.BlockSpec((tm, d), lambda i, j: (i, 0)),
        out_shape=jax.ShapeDtypeStruct((t, d), F32),
        scratch_shapes=[pltpu.VMEM((tm, d), BF16)],
        compiler_params=_cparams(("parallel", "arbitrary"), vmem),
        name="ffn_final" if final else "ffn",
    )(*args)


def _proj_kernel(x_ref, mod_ref, nw_ref, wa_ref, wb_ref, wf_ref, z_ref, f_ref, h_ref, *, mod_base, n_a):
    j = pl.program_id(1)

    @pl.when(j == 0)
    def _():
        sh = mod_ref[0, mod_base:mod_base + 1, :]
        sc = mod_ref[0, mod_base + 1:mod_base + 2, :]
        h = _norm_mod(x_ref[...], nw_ref[...], sh, sc).astype(BF16)
        h_ref[...] = h
        f_ref[...] = _dot(h, wf_ref[...])

    @pl.when(j < n_a)
    def _():
        z_ref[...] = _dot(h_ref[...], wa_ref[...]).astype(BF16)

    @pl.when(j >= n_a)
    def _():
        z_ref[...] = _dot(h_ref[...], wb_ref[...]).astype(BF16)


def _proj(x2, mod3, norm_w4, w_a, w_b, w_f, *, layer, seq, mod_base, tm, tn):
    t, d = x2.shape
    n_a = w_a.shape[2] // tn
    nz = w_a.shape[2] + w_b.shape[2]
    per_b = seq // tm
    vmem = 2 * tm * d * 4 + tm * d * 2 + 4 * d * tn * 2 + 2 * tm * tn * 2 + 2 * (tm * LANES * 4 + d * LANES * 2) \
        + 2 * tm * tn * 4 + tm * d * 4
    return pl.pallas_call(
        functools.partial(_proj_kernel, mod_base=mod_base, n_a=n_a),
        grid=(t // tm, nz // tn),
        in_specs=[pl.BlockSpec((tm, d), lambda i, j: (i, 0)),
                  pl.BlockSpec((1, N_MOD, d), lambda i, j: (i // per_b, 0, 0)),
                  pl.BlockSpec((None, None, 1, d), lambda i, j: (layer, 1, 0, 0)),
                  pl.BlockSpec((None, d, tn), lambda i, j: (layer, 0, jnp.minimum(j, n_a - 1))),
                  pl.BlockSpec((None, d, tn), lambda i, j: (layer, 0, jnp.maximum(j - n_a, 0))),
                  pl.BlockSpec((None, d, LANES), lambda i, j: (layer, 0, 0))],
        out_specs=[pl.BlockSpec((tm, tn), lambda i, j: (i, j)),
                   pl.BlockSpec((tm, LANES), lambda i, j: (i, 0))],
        out_shape=[jax.ShapeDtypeStruct((t, nz), BF16),
                   jax.ShapeDtypeStruct((t, LANES), F32)],
        scratch_shapes=[pltpu.VMEM((tm, d), BF16)],
        compiler_params=_cparams(("parallel", "arbitrary"), vmem),
        name="in_proj",
    )(x2, mod3, norm_w4, w_a, w_b, w_f)


def _split3(x):
    hi = x.astype(BF16)
    r1 = x - hi.astype(F32)
    mid = r1.astype(BF16)
    lo = (r1 - mid.astype(F32)).astype(BF16)
    return hi, mid, lo


def _cumf_kernel(f_ref, fb_ref, eq_ref, ek_ref, cq_ref, ck_ref, qa_ref, ka_ref, *, chunk):
    s_len = f_ref.shape[0]
    row = lax.broadcasted_iota(jnp.int32, (chunk, chunk), 0)
    col = lax.broadcasted_iota(jnp.int32, (chunk, chunk), 1)
    tri = jnp.where(row >= col, 1.0, 0.0).astype(BF16)
    carry = jnp.zeros((1, LANES), F32)
    for ci in range(s_len // chunk):
        rows = slice(ci * chunk, (ci + 1) * chunk)
        x = f_ref[rows, :] + fb_ref[...]
        ls = jnp.minimum(x, 0.0) - jnp.log(1.0 + jnp.exp(-jnp.abs(x)))
        hi, mid, lo = _split3(ls)
        cs = _dot(tri, hi) + _dot(tri, mid) + _dot(tri, lo) + carry
        carry = cs[chunk - 1:chunk, :]
        pieces = jnp.concatenate(_split3(cs * LOG2E), axis=1)
        qa_ref[0, rows, :] = (_dot(pieces, eq_ref[...]) + cq_ref[...]).astype(BF16)
        ka_ref[0, rows, :] = (_dot(pieces, ek_ref[...]) + ck_ref[...]).astype(BF16)


def _fox_aug_tables(heads):
    n = heads * HEAD_DIM
    eq = jnp.zeros((3 * LANES, n), F32)
    ek = jnp.zeros((3 * LANES, n), F32)
    cq = jnp.zeros((1, n), F32)
    ck = jnp.zeros((1, n), F32)
    h = jnp.arange(heads)
    for piece in range(3):
        eq = eq.at[piece * LANES + h, h * HEAD_DIM + piece].set(1.0)
        ek = ek.at[piece * LANES + h, h * HEAD_DIM + 3 + piece].set(-1.0)
        cq = cq.at[0, h * HEAD_DIM + 3 + piece].set(1.0)
        ck = ck.at[0, h * HEAD_DIM + piece].set(1.0)
    return eq.astype(BF16), ek.astype(BF16), cq, ck


def _cumf(fa, f_bias_pad, aug_tables, *, bsz, seq, heads):
    chunk = 256 if seq % 256 == 0 else seq
    n = heads * HEAD_DIM
    eq, ek, cq, ck = aug_tables
    e_spec = pl.BlockSpec((3 * LANES, n), lambda b: (0, 0))
    c_spec = pl.BlockSpec((1, n), lambda b: (0, 0))
    o_spec = pl.BlockSpec((1, seq, n), lambda b: (b, 0, 0))
    return pl.pallas_call(
        functools.partial(_cumf_kernel, chunk=chunk),
        grid=(bsz,),
        in_specs=[pl.BlockSpec((seq, LANES), lambda b: (b, 0)),
                  pl.BlockSpec((1, LANES), lambda b: (0, 0)), e_spec, e_spec, c_spec, c_spec],
        out_specs=[o_spec, o_spec],
        out_shape=[jax.ShapeDtypeStruct((bsz, seq, n), BF16)] * 2,
        compiler_params=_cparams(("parallel",), 8 * seq * n * 2 + 8 * seq * LANES * 4),
        name="fox_cum_logf",
    )(fa, f_bias_pad, eq, ek, cq, ck)


def _fox_kernel(q_ref, k_ref, v_ref, qa_ref, ka_ref, o_ref, *, tile, scale):
    seq = q_ref.shape[1]
    qs = (q_ref[0].astype(F32) * (scale * LOG2E)).astype(BF16)
    qp = jnp.concatenate([qs, qa_ref[0]], axis=1)
    kp = jnp.concatenate([k_ref[0], ka_ref[0]], axis=1)
    r = lax.broadcasted_iota(jnp.int32, (tile, tile), 0)
    c = lax.broadcasted_iota(jnp.int32, (tile, tile), 1)
    causal = c <= r
    for i in range(seq // tile):
        qi = qp[i * tile:(i + 1) * tile]
        m = l = acc = None
        for j in range(i + 1):
            keys = slice(j * tile, (j + 1) * tile)
            s = lax.dot_general(qi, kp[keys], (((1,), (1,)), ((), ())), preferred_element_type=F32)
            if j == i:
                s = jnp.where(causal, s, NEG_BIG)
            m_tile = jnp.max(s, axis=-1, keepdims=True)
            if j == 0:
                m = m_tile
                p = jnp.exp2(s - m)
                l = jnp.sum(p, axis=-1, keepdims=True)
                acc = _dot(p.astype(BF16), v_ref[0, keys, :])
            else:
                m_new = jnp.maximum(m, m_tile)
                alpha = jnp.exp2(m - m_new)
                p = jnp.exp2(s - m_new)
                l = alpha * l + jnp.sum(p, axis=-1, keepdims=True)
                acc = alpha * acc + _dot(p.astype(BF16), v_ref[0, keys, :])
                m = m_new
        o_ref[0, i * tile:(i + 1) * tile, :] = (acc / l).astype(BF16)


def _fox(z3, q_aug, k_aug, *, heads, q_blk, k_blk, v_blk, tile):
    bsz, seq, _ = z3.shape
    width = heads * HEAD_DIM
    head_spec = lambda blk: pl.BlockSpec((1, seq, HEAD_DIM), lambda b, h: (b, 0, blk + h))
    return pl.pallas_call(
        functools.partial(_fox_kernel, tile=tile, scale=HEAD_DIM ** -0.5),
        grid=(bsz, heads),
        in_specs=[head_spec(q_blk), head_spec(k_blk), head_spec(v_blk), head_spec(0), head_spec(0)],
        out_specs=head_spec(0),
        out_shape=jax.ShapeDtypeStruct((bsz, seq, width), BF16),
        compiler_params=_cparams(("parallel", "parallel"), 40 << 20),
        name="fox_attention",
    )(z3, z3, z3, q_aug, k_aug)


def _ssm_disc_kernel(are_ref, aim_ref, ldt_ref, bre_ref, bim_ref, abr_ref, abi_ref, bbr_ref, bbi_ref):
    a_re = are_ref[...]
    a_im = aim_ref[...]
    dt = jnp.exp(ldt_ref[...])
    mag = jnp.exp(a_re * dt)
    ab_re = mag * jnp.cos(a_im * dt)
    ab_im = mag * jnp.sin(a_im * dt)
    den = a_re * a_re + a_im * a_im
    num_re = ab_re - 1.0
    coef_re = (num_re * a_re + ab_im * a_im) / den
    coef_im = (ab_im * a_re - num_re * a_im) / den
    b_re = bre_ref[...]
    b_im = bim_ref[...]
    abr_ref[...] = ab_re
    abi_ref[...] = ab_im
    bbr_ref[...] = coef_re * b_re - coef_im * b_im
    bbi_ref[...] = coef_re * b_im + coef_im * b_re


def _ssm_disc(a_re, a_im, log_dt, b_re, b_im):
    g, p, n = b_re.shape
    e = p * n
    expand = lambda a: jnp.repeat(a, n, axis=-1)
    args = (expand(a_re), expand(a_im), jnp.broadcast_to(log_dt[:, None], (g, e)),
            b_re.reshape(g, e), b_im.reshape(g, e))
    spec = pl.BlockSpec((g, e), lambda: (0, 0))
    outs = pl.pallas_call(
        _ssm_disc_kernel,
        in_specs=[spec] * 5,
        out_specs=[spec] * 4,
        out_shape=[jax.ShapeDtypeStruct((g, e), F32)] * 4,
        name="ssm_discretize",
    )(*args)
    ab_re, ab_im, bb_re, bb_im = outs
    return ab_re[:, ::n], ab_im[:, ::n], bb_re.reshape(g, p, n), bb_im.reshape(g, p, n)


def _gelu_tanh(x):
    return 0.5 * x * (1.0 + jnp.tanh(math.sqrt(2.0 / math.pi) * (x + 0.044715 * (x * x * x))))


def _ssm_kernel(u_ref, wb_ref, wc_ref, ar_ref, ai_ref, d_ref, o_ref, bu_ref, st_ref, *, tc, half):
    ti = pl.program_id(1)

    @pl.when(ti == 0)
    def _():
        st_ref[...] = jnp.zeros_like(st_ref)

    u = u_ref[...]
    bu_ref[...] = _dot(u, wb_ref[0])
    ar = ar_ref[0]
    ai = ai_ref[0]

    def step(t, carry):
        xr, xi = carry
        r = pl.multiple_of(t * SUBLANES, SUBLANES)
        bur = bu_ref[pl.ds(r, SUBLANES), 0:half]
        bui = bu_ref[pl.ds(r, SUBLANES), half:2 * half]
        nxr = ar * xr - ai * xi + bur
        nxi = ar * xi + ai * xr + bui
        bu_ref[pl.ds(r, SUBLANES), 0:half] = nxr
        bu_ref[pl.ds(r, SUBLANES), half:2 * half] = nxi
        return nxr, nxi

    xr, xi = lax.fori_loop(0, tc, step, (st_ref[:, 0:half], st_ref[:, half:2 * half]), unroll=8)
    st_ref[:, 0:half] = xr
    st_ref[:, half:2 * half] = xi
    y = _dot(bu_ref[...].astype(BF16), wc_ref[0])
    y = y + d_ref[...] * u.astype(F32)
    o_ref[...] = _gelu_tanh(y).astype(BF16)


def _ssm(u_tm, wb, wc, ar, ai, d_skip, *, seq, tc):
    rows, width = u_tm.shape
    n_chunk = width // LANES
    half = wb.shape[2] // 2
    rt = tc * SUBLANES
    return pl.pallas_call(
        functools.partial(_ssm_kernel, tc=tc, half=half),
        grid=(n_chunk, seq // tc),
        in_specs=[pl.BlockSpec((rt, LANES), lambda c, t: (t, c)),
                  pl.BlockSpec((1, LANES, 2 * half), lambda c, t: (c, 0, 0)),
                  pl.BlockSpec((1, 2 * half, LANES), lambda c, t: (c, 0, 0)),
                  pl.BlockSpec((1, SUBLANES, half), lambda c, t: (c, 0, 0)),
                  pl.BlockSpec((1, SUBLANES, half), lambda c, t: (c, 0, 0)),
                  pl.BlockSpec((1, LANES), lambda c, t: (0, c))],
        out_specs=pl.BlockSpec((rt, LANES), lambda c, t: (t, c)),
        out_shape=jax.ShapeDtypeStruct((rows, width), BF16),
        scratch_shapes=[pltpu.VMEM((rt, 2 * half), F32), pltpu.VMEM((SUBLANES, 2 * half), F32)],
        compiler_params=_cparams(("parallel", "arbitrary"), 3 * rt * 2 * half * 4 + (8 << 20)),
        name="s5_ssm",
    )(u_tm, wb, wc, ar, ai, d_skip.reshape(1, width))


def _ssm_weights(ab_re, ab_im, bb_re, bb_im, c_re, c_im):
    g, p, n = bb_re.shape
    gpc = LANES // n
    nch = g // gpc
    eye = jnp.eye(gpc, dtype=F32)
    bb = jnp.stack([bb_re, bb_im], axis=0).reshape(2, nch, gpc, p, n)
    wb = jnp.einsum('rcgpn,gh->cgnrhp', bb, eye).reshape(nch, gpc * n, 2 * gpc * p)
    cc = jnp.stack([c_re, -c_im], axis=0).reshape(2, nch, gpc, n, p)
    wc = jnp.einsum('rcgnp,gh->crhpgn', cc, eye).reshape(nch, 2 * gpc * p, gpc * n)
    ar = jnp.broadcast_to(ab_re.reshape(nch, 1, gpc * p), (nch, SUBLANES, gpc * p))
    ai = jnp.broadcast_to(ab_im.reshape(nch, 1, gpc * p), (nch, SUBLANES, gpc * p))
    return wb.astype(BF16), wc.astype(BF16), ar, ai


def _ret_kernel(q_ref, k_ref, v_ref, g_ref, cos_ref, sin_ref, intra_ref, qd_ref, kd_ref, cd_ref, gn_ref,
                o_ref, *, chunk, scale):
    seq = q_ref.shape[1]
    half = HEAD_DIM // 2
    cs = cos_ref[...]
    sn = sin_ref[...]

    def rot(t):
        return t * cs + pltpu.roll(t, half, 1) * sn

    qr = rot(q_ref[0].astype(F32))
    kr = rot(k_ref[0].astype(F32)) * scale
    qb = qr.astype(BF16)
    kb = kr.astype(BF16)
    intra = intra_ref[0]
    qd = qd_ref[0]
    kd = kd_ref[0]
    cd = cd_ref[0, 0:1, :]
    gn = gn_ref[...]
    n_chunks = seq // chunk
    state = None
    for ci in range(n_chunks):
        rows = slice(ci * chunk, (ci + 1) * chunk)
        vc = v_ref[0, rows, :]
        scores = lax.dot_general(qb[rows], kb[rows], (((1,), (1,)), ((), ())),
                                 preferred_element_type=F32) * intra
        o = _dot(scores.astype(BF16), vc)
        if state is not None:
            o = o + _dot((qr[rows] * qd).astype(BF16), state.astype(BF16))
        if ci + 1 < n_chunks:
            kv = lax.dot_general((kr[rows] * kd).astype(BF16), vc, (((0,), (0,)), ((), ())),
                                 preferred_element_type=F32)
            state = kv if state is None else cd * state + kv
        mu = jnp.mean(o, axis=-1, keepdims=True)
        oc = o - mu
        var = jnp.mean(oc * oc, axis=-1, keepdims=True)
        on = oc * lax.rsqrt(var + EPS)
        gr = g_ref[0, rows, :].astype(F32)
        o_ref[0, rows, :] = ((on * gn) * (gr * jax.nn.sigmoid(gr))).astype(BF16)


def _ret_tables(seq, heads, chunk):
    half = HEAD_DIM // 2
    pos = jnp.arange(seq, dtype=F32)
    inv_freq = 1.0 / (ROPE_BASE ** jnp.linspace(0.0, 1.0, half, dtype=F32))
    ang = pos[:, None] * inv_freq[None, :]
    cos = jnp.cos(ang)
    sin = jnp.sin(ang)
    cos2 = jnp.concatenate([cos, cos], axis=-1)
    sin2 = jnp.concatenate([-sin, sin], axis=-1)
    log_gamma = jnp.log(1.0 - jnp.exp2(-RET_DECAY_BASE - jnp.arange(heads, dtype=F32)))
    idx = jnp.arange(chunk, dtype=F32)
    rel = idx[:, None] - idx[None, :]
    intra = jnp.where(rel[None] >= 0, jnp.exp(jnp.maximum(rel, 0.0)[None] * log_gamma[:, None, None]), 0.0)
    q_decay = jnp.exp((idx + 1.0)[None, :] * log_gamma[:, None])
    k_decay = jnp.exp((chunk - 1.0 - idx)[None, :] * log_gamma[:, None])
    chunk_decay = jnp.exp(chunk * log_gamma)
    qd = jnp.broadcast_to(q_decay[:, :, None], (heads, chunk, HEAD_DIM))
    kd = jnp.broadcast_to(k_decay[:, :, None], (heads, chunk, HEAD_DIM))
    cdec = jnp.broadcast_to(chunk_decay[:, None, None], (heads, SUBLANES, HEAD_DIM))
    return cos2, sin2, intra, qd, kd, cdec


def _ret(z3, tables, gn_w, *, heads, q_blk, k_blk, v_blk, g_blk, chunk):
    bsz, seq, _ = z3.shape
    width = heads * HEAD_DIM
    cos2, sin2, intra, qd, kd, cdec = tables
    head_spec = lambda blk: pl.BlockSpec((1, seq, HEAD_DIM), lambda h, b: (b, 0, blk + h))
    tab_spec = pl.BlockSpec((seq, HEAD_DIM), lambda h, b: (0, 0))
    per_head = lambda rows, cols: pl.BlockSpec((1, rows, cols), lambda h, b: (h, 0, 0))
    return pl.pallas_call(
        functools.partial(_ret_kernel, chunk=chunk, scale=HEAD_DIM ** -0.5),
        grid=(heads, bsz),
        in_specs=[head_spec(q_blk), head_spec(k_blk), head_spec(v_blk), head_spec(g_blk),
                  tab_spec, tab_spec, per_head(chunk, chunk), per_head(chunk, HEAD_DIM),
                  per_head(chunk, HEAD_DIM), per_head(SUBLANES, HEAD_DIM),
                  pl.BlockSpec((1, HEAD_DIM), lambda h, b: (0, h))],
        out_specs=pl.BlockSpec((1, seq, HEAD_DIM), lambda h, b: (b, 0, h)),
        out_shape=jax.ShapeDtypeStruct((bsz, seq, width), BF16),
        compiler_params=_cparams(("parallel", "parallel"), 40 << 20),
        name="retention",
    )(z3, z3, z3, z3, cos2, sin2, intra, qd, kd, cdec, gn_w.reshape(1, width))


def _merge_kernel(yf_ref, yg_ref, yr_ref, g0_ref, g1_ref, g2_ref, wb_ref, bg_ref, gw_ref, gb_ref,
                  o_ref, ys_ref):
    j = pl.program_id(1)

    @pl.when(j == 0)
    def _():
        yg = yg_ref[...]
        t = _dot(yg, gw_ref[...]) + gb_ref[...]
        ys_ref[...] = (yg.astype(F32) * jax.nn.sigmoid(t)).astype(BF16)

    def gate(g_ref, k):
        return jax.nn.sigmoid(g_ref[...].astype(F32) + bg_ref[k:k + 1, :])

    acc = gate(g0_ref, 0) * _dot(yf_ref[...], wb_ref[0])
    acc += gate(g1_ref, 1) * _dot(ys_ref[...], wb_ref[1])
    acc += gate(g2_ref, 2) * _dot(yr_ref[...], wb_ref[2])
    o_ref[...] = acc.astype(BF16)


def _merge(y_fox, y_g, y_ret, z, w_branch, b_gate, glu_w, glu_b, *, layer, gate_blk, tm, tn):
    t, width = y_fox.shape
    depth, _, _, d = w_branch.shape
    y_spec = pl.BlockSpec((tm, width), lambda i, j: (i, 0))
    nb = d // tn
    g_spec = lambda k: pl.BlockSpec((tm, tn), lambda i, j: (i, gate_blk * LANES // tn + k * nb + j))
    vmem = 3 * 2 * tm * width * 2 + 3 * 2 * tm * tn * 2 + 2 * 3 * width * tn * 2 + 2 * width * width * 2 \
        + tm * width * 2 + 2 * tm * tn * 2 + 6 * tm * tn * 4 + 2 * tm * width * 4
    return pl.pallas_call(
        _merge_kernel,
        grid=(t // tm, nb),
        in_specs=[y_spec, y_spec, y_spec, g_spec(0), g_spec(1), g_spec(2),
                  pl.BlockSpec((None, 3, width, tn), lambda i, j: (layer, 0, 0, j)),
                  pl.BlockSpec((None, 3, tn), lambda i, j: (layer, 0, j)),
                  pl.BlockSpec((None, width, width), lambda i, j: (layer, 0, 0)),
                  pl.BlockSpec((None, 1, width), lambda i, j: (layer, 0, 0))],
        out_specs=pl.BlockSpec((tm, tn), lambda i, j: (i, j)),
        out_shape=jax.ShapeDtypeStruct((t, d), BF16),
        scratch_shapes=[pltpu.VMEM((tm, width), BF16)],
        compiler_params=_cparams(("parallel", "arbitrary"), vmem),
        name="branch_merge",
    )(y_fox, y_g, y_ret, z, z, z, w_branch, b_gate, glu_w, glu_b.reshape(depth, 1, width))


def _outproj_kernel(m_ref, w_ref, x_ref, mod_ref, o_ref, *, gate_row):
    y = _dot(m_ref[...], w_ref[...])
    o_ref[...] = x_ref[...] + mod_ref[0, gate_row:gate_row + 1, :] * y


def _outproj(merged, w_out, x2, mod3, *, layer, seq, gate_row, tm, tn):
    t, d = x2.shape
    per_b = seq // tm
    vmem = 2 * tm * d * 2 + 2 * d * tn * 2 + 4 * tm * tn * 4 + 2 * tm * tn * 4
    return pl.pallas_call(
        functools.partial(_outproj_kernel, gate_row=gate_row),
        grid=(t // tm, d // tn),
        in_specs=[pl.BlockSpec((tm, d), lambda i, j: (i, 0)),
                  pl.BlockSpec((None, d, tn), lambda i, j: (layer, 0, j)),
                  pl.BlockSpec((tm, tn), lambda i, j: (i, j)),
                  pl.BlockSpec((1, N_MOD, tn), lambda i, j: (i // per_b, 0, j))],
        out_specs=pl.BlockSpec((tm, tn), lambda i, j: (i, j)),
        out_shape=jax.ShapeDtypeStruct((t, d), F32),
        compiler_params=_cparams(("parallel", "arbitrary"), vmem),
        name="out_proj",
    )(merged, w_out, x2, mod3)


def _tiles(seq, d, f):
    tm = min(1024, seq)
    tf = 512
    fp = -(-f // tf) * tf
    width = d // 2
    tn_proj = next(c for c in (1024, 512, 256, 128) if (3 * width) % c == 0 and (5 * width + 3 * d) % c == 0)
    return dict(tm=tm, tm_ffn=min(512, seq), tf=tf, fp=fp, tn=min(1024, d), tn_proj=tn_proj,
                att_tile=min(512, seq), ret_chunk=min(512, seq), tc=min(256, seq))


def kernel(x, c, ada_w, ada_b, norm_w, final_norm_w, ffn_w1, ffn_w3, ffn_w2, w_in, fox_f_bias, ssm_A_re, ssm_A_im, ssm_log_dt, ssm_B_re, ssm_B_im, ssm_C_re, ssm_C_im, ssm_D, glu_w, glu_b, ret_gn_w, w_branch, b_gate, w_out):
    bsz, seq, d = x.shape
    depth = ada_w.shape[0]
    width = d // 2
    heads = width // HEAD_DIM
    f = ffn_w1.shape[-1]
    assert bsz == SUBLANES, "the S5 kernel maps the batch onto the 8 sublanes of a vreg"
    assert heads <= SUBLANES and width % LANES == 0 and seq % RET_CHUNK == 0
    cfg = _tiles(seq, d, f)
    tm, tf, fp, tn = cfg["tm"], cfg["tf"], cfg["fp"], cfg["tn"]
    t = bsz * seq

    pad_f = fp - f
    w1 = jnp.pad(ffn_w1, ((0, 0), (0, 0), (0, 0), (0, pad_f))).astype(BF16)
    w3 = jnp.pad(ffn_w3, ((0, 0), (0, 0), (0, 0), (0, pad_f))).astype(BF16)
    w2 = jnp.pad(ffn_w2, ((0, 0), (0, 0), (0, pad_f), (0, 0))).astype(BF16)
    f_lo = 3 * width
    f_hi = f_lo + heads
    w_a = w_in[:, :, :f_lo].astype(BF16)
    w_b = w_in[:, :, f_hi:].astype(BF16)
    w_f = jnp.pad(w_in[:, :, f_lo:f_hi], ((0, 0), (0, 0), (0, LANES - heads))).astype(BF16)
    f_bias = jnp.pad(fox_f_bias, ((0, 0), (0, LANES - heads)))
    glu_wb = glu_w.astype(BF16)
    w_br = w_branch.astype(BF16)
    w_o = w_out.astype(BF16)
    wblk = width // LANES
    blk = dict(qa=0, ka=wblk, va=2 * wblk, us=3 * wblk, qr=4 * wblk, kr=5 * wblk, vr=6 * wblk,
               gr=7 * wblk, gl=8 * wblk)
    ret_tables = _ret_tables(seq, heads, cfg["ret_chunk"])
    aug_tables = _fox_aug_tables(heads)

    norm_w4 = norm_w.reshape(depth, 3, 1, d)
    x2 = x.reshape(t, d)
    for l in range(depth):
        mod3 = _ada(c, ada_w, ada_b, l).reshape(bsz, N_MOD, d)
        x2 = _ffn(x2, mod3, norm_w4, w1, w3, w2, None, layer=l, which=0, seq=seq, tm=cfg["tm_ffn"], tf=tf)
        z, fa = _proj(x2, mod3, norm_w4, w_a, w_b, w_f, layer=l, seq=seq, mod_base=3, tm=tm, tn=cfg["tn_proj"])
        z3 = z.reshape(bsz, seq, z.shape[1])
        q_aug, k_aug = _cumf(fa, f_bias[l:l + 1], aug_tables, bsz=bsz, seq=seq, heads=heads)
        y_fox = _fox(z3, q_aug, k_aug, heads=heads,
                     q_blk=blk["qa"], k_blk=blk["ka"], v_blk=blk["va"], tile=cfg["att_tile"])
        ab_re, ab_im, bb_re, bb_im = _ssm_disc(ssm_A_re[l], ssm_A_im[l], ssm_log_dt[l], ssm_B_re[l], ssm_B_im[l])
        wb, wc, ar, ai = _ssm_weights(ab_re, ab_im, bb_re, bb_im, ssm_C_re[l], ssm_C_im[l])
        u_tm = jnp.transpose(z3[:, :, blk["us"] * LANES:(blk["us"] + wblk) * LANES], (1, 0, 2)).reshape(t, width)
        y_tm = _ssm(u_tm, wb, wc, ar, ai, ssm_D[l], seq=seq, tc=cfg["tc"])
        y_g = jnp.transpose(y_tm.reshape(seq, bsz, width), (1, 0, 2)).reshape(t, width)
        y_ret = _ret(z3, ret_tables, ret_gn_w[l], heads=heads,
                     q_blk=blk["qr"], k_blk=blk["kr"], v_blk=blk["vr"], g_blk=blk["gr"],
                     chunk=cfg["ret_chunk"])
        merged = _merge(y_fox.reshape(t, width), y_g, y_ret.reshape(t, width), z, w_br, b_gate,
                        glu_wb, glu_b, layer=l, gate_blk=blk["gl"], tm=tm, tn=min(512, d))
        x2 = _outproj(merged, w_o, x2, mod3, layer=l, seq=seq, gate_row=5, tm=tm, tn=tn)
        x2 = _ffn(x2, mod3, norm_w4, w1, w3, w2, final_norm_w if l == depth - 1 else None,
                  layer=l, which=1, seq=seq, tm=cfg["tm_ffn"], tf=tf)
    return x2.reshape(bsz, seq, d)
```
